```python
import math
import jax, jax.numpy as jnp
from jax import lax
import numpy as np

D_MODEL = 1024
BATCH = 8
SEQ = 2048
DEPTH = 2

CONV_CH = D_MODEL
CONV_K = 31
SSM_D_INNER = 2 * D_MODEL
SSM_HEADDIM = 64
SSM_HEADS = SSM_D_INNER // SSM_HEADDIM
SSM_GROUPS = 4
SSM_STATE = 128
SSM_CONV_K = 4
SSM_CHUNK = 128
SSM_XBC = SSM_D_INNER + 2 * SSM_GROUPS * SSM_STATE
N_BRANCHES = 2
IN_SPLIT = (CONV_CH, CONV_CH, SSM_D_INNER, SSM_XBC, SSM_HEADS, N_BRANCHES * D_MODEL)
D_IN_PROJ = 2 * CONV_CH + SSM_D_INNER + SSM_XBC + SSM_HEADS + N_BRANCHES * D_MODEL
FF_DENSE = 2816
N_EXPERTS = 8
TOP_K = 2
FF_EXPERT = 3584
N_DENSE = (DEPTH + 1) // 2
N_MOE = DEPTH // 2
LN_EPS = 1e-5
RMS_EPS = 1e-5
ALPHA = (2 * DEPTH) ** 0.25
BETA = (8 * DEPTH) ** -0.25

kernel_name = "hybrid_conformer_conv_mamba2_deepnorm_moe"


def layer_norm(x, g, b):
    xf = x.astype(jnp.float32)
    mu = jnp.mean(xf, -1, keepdims=True)
    var = jnp.mean(jnp.square(xf - mu), -1, keepdims=True)
    return ((xf - mu) * lax.rsqrt(var + LN_EPS) * g.astype(jnp.float32) + b.astype(jnp.float32)).astype(x.dtype)


def causal_depthwise_conv(u, w, b):
    k = w.shape[0]
    out = lax.conv_general_dilated(
        u, w[:, None, :], window_strides=(1,), padding=[(k - 1, 0)],
        dimension_numbers=("NWC", "WIO", "NWC"), feature_group_count=u.shape[-1])
    return out + b


def gated_group_rmsnorm(y, z, w):
    h = (y * jax.nn.silu(z)).astype(jnp.float32)
    hg = h.reshape(h.shape[:-1] + (SSM_GROUPS, SSM_D_INNER // SSM_GROUPS))
    hg = hg * lax.rsqrt(jnp.mean(jnp.square(hg), -1, keepdims=True) + RMS_EPS)
    return (hg.reshape(h.shape) * w.astype(jnp.float32)).astype(y.dtype)


def ssd_chunked(xs, dt, a, bm, cm):
    f32 = jnp.float32
    bsz, seqlen = xs.shape[0], xs.shape[1]
    nc = seqlen // SSM_CHUNK
    r = SSM_HEADS // SSM_GROUPS
    x = (xs.astype(f32) * dt[..., None]).reshape(bsz, nc, SSM_CHUNK, SSM_GROUPS, r, SSM_HEADDIM)
    da = (dt * a.astype(f32)).reshape(bsz, nc, SSM_CHUNK, SSM_GROUPS, r)
    a_cum = jnp.cumsum(jnp.moveaxis(da, 2, -1), axis=-1)
    bc = bm.astype(f32).reshape(bsz, nc, SSM_CHUNK, SSM_GROUPS, SSM_STATE)
    cc = cm.astype(f32).reshape(bsz, nc, SSM_CHUNK, SSM_GROUPS, SSM_STATE)
    pos = jnp.arange(SSM_CHUNK)
    causal = pos[:, None] >= pos[None, :]
    seg = a_cum[..., :, None] - a_cum[..., None, :]
    decay = jnp.exp(jnp.where(causal, seg, -jnp.inf))
    cb = jnp.einsum("bclgn,bcsgn->bcgls", cc, bc)
    y_diag = jnp.einsum("bcgls,bcgrls,bcsgrp->bclgrp", cb, decay, x)
    decay_to_end = jnp.exp(a_cum[..., -1:] - a_cum)
    states = jnp.einsum("bclgn,bcgrl,bclgrp->bcgrpn", bc, decay_to_end, x)
    chunk_decay = jnp.exp(a_cum[..., -1])

    def step(h, inp):
        s_c, d_c = inp
        return h * d_c[..., None, None] + s_c, h

    h0 = jnp.zeros((bsz, SSM_GROUPS, r, SSM_HEADDIM, SSM_STATE), f32)
    _, prev = lax.scan(step, h0, (jnp.moveaxis(states, 1, 0), jnp.moveaxis(chunk_decay, 1, 0)))
    prev = jnp.moveaxis(prev, 0, 1)
    y_off = jnp.einsum("bclgn,bcgrpn,bcgrl->bclgrp", cc, prev, jnp.exp(a_cum))
    return (y_diag + y_off).reshape(bsz, seqlen, SSM_HEADS, SSM_HEADDIM)


def hybrid_mixer(x, w_in, conv_dw_w, conv_dw_b, conv_ln_g, conv_ln_b, conv_w_out,
                 ssm_conv_w, ssm_conv_b, ssm_dt_bias, ssm_a_log, ssm_d, ssm_norm_w,
                 ssm_w_out, w_out):
    bsz, seqlen, _ = x.shape
    proj = x @ w_in
    cuts = np.cumsum(IN_SPLIT)[:-1].tolist()
    cv, cg, z, xbc, dt_raw, gates = jnp.split(proj, cuts, axis=-1)
    u = cv * jax.nn.sigmoid(cg)
    u = causal_depthwise_conv(u, conv_dw_w, conv_dw_b)
    u = jax.nn.silu(layer_norm(u, conv_ln_g, conv_ln_b))
    y_conv = u @ conv_w_out
    xbc = jax.nn.silu(causal_depthwise_conv(xbc, ssm_conv_w, ssm_conv_b))
    xs, bm, cm = jnp.split(xbc, [SSM_D_INNER, SSM_D_INNER + SSM_GROUPS * SSM_STATE], axis=-1)
    xs = xs.reshape(bsz, seqlen, SSM_HEADS, SSM_HEADDIM)
    bm = bm.reshape(bsz, seqlen, SSM_GROUPS, SSM_STATE)
    cm = cm.reshape(bsz, seqlen, SSM_GROUPS, SSM_STATE)
    dt = jax.nn.softplus((dt_raw + ssm_dt_bias).astype(jnp.float32))
    a = -jnp.exp(ssm_a_log.astype(jnp.float32))
    y = ssd_chunked(xs, dt, a, bm, cm)
    y = y + ssm_d.astype(jnp.float32)[:, None] * xs.astype(jnp.float32)
    y = y.reshape(bsz, seqlen, SSM_D_INNER).astype(x.dtype)
    y = gated_group_rmsnorm(y, z, ssm_norm_w)
    y_ssm = y @ ssm_w_out
    g_conv, g_ssm = jnp.split(gates, 2, axis=-1)
    h = jax.nn.sigmoid(g_conv) * y_conv + jax.nn.sigmoid(g_ssm) * y_ssm
    return h @ w_out


def swiglu(t, w_gate, w_up, w_down):
    return (jax.nn.silu(t @ w_gate) * (t @ w_up)) @ w_down


def moe_swiglu(x, w_router, w_gate, w_up, w_down):
    bsz, seqlen, d = x.shape
    t = x.reshape(bsz * seqlen, d)
    logits = (t @ w_router).astype(jnp.float32)
    top_vals, top_idx = lax.top_k(logits, TOP_K)
    top_w = jax.nn.softmax(top_vals, axis=-1)
    combine = jnp.sum(jax.nn.one_hot(top_idx, N_EXPERTS, dtype=jnp.float32) * top_w[..., None], axis=1)
    combine = combine.astype(x.dtype)
    y = jnp.zeros_like(t)
    for e in range(N_EXPERTS):
        y = y + combine[:, e:e + 1] * swiglu(t, w_gate[e], w_up[e], w_down[e])
    return y.reshape(bsz, seqlen, d)


def setup_inputs(seed: int = 0) -> dict:
    key = jax.random.key(seed)
    ks = iter(jax.random.split(key, 40))

    def nrm(shape, scale):
        return jax.random.normal(next(ks), shape, jnp.float32) * scale

    def near_one(shape):
        return 1.0 + nrm(shape, 0.02)

    dt_init = jnp.exp(jax.random.uniform(next(ks), (DEPTH, SSM_HEADS), jnp.float32,
                                         minval=math.log(1e-3), maxval=math.log(1e-1)))
    dt_bias = dt_init + jnp.log(-jnp.expm1(-dt_init))
    a_log = jnp.log(jax.random.uniform(next(ks), (DEPTH, SSM_HEADS), jnp.float32, minval=1.0, maxval=16.0))
    return {
        "x": nrm((BATCH, SEQ, D_MODEL), 1.0),
        "mix_w_in": nrm((DEPTH, D_MODEL, D_IN_PROJ), D_MODEL ** -0.5),
        "conv_dw_w": nrm((DEPTH, CONV_K, CONV_CH), CONV_K ** -0.5),
        "conv_dw_b": nrm((DEPTH, CONV_CH), 0.02),
        "conv_ln_g": near_one((DEPTH, CONV_CH)),
        "conv_ln_b": nrm((DEPTH, CONV_CH), 0.02),
        "conv_w_out": nrm((DEPTH, CONV_CH, D_MODEL), CONV_CH ** -0.5),
        "ssm_conv_w": nrm((DEPTH, SSM_CONV_K, SSM_XBC), SSM_CONV_K ** -0.5),
        "ssm_conv_b": nrm((DEPTH, SSM_XBC), 0.02),
        "ssm_dt_bias": dt_bias,
        "ssm_a_log": a_log,
        "ssm_d": near_one((DEPTH, SSM_HEADS)),
        "ssm_norm_w": near_one((DEPTH, SSM_D_INNER)),
        "ssm_w_out": nrm((DEPTH, SSM_D_INNER, D_MODEL), SSM_D_INNER ** -0.5),
        "mix_w_out": nrm((DEPTH, D_MODEL, D_MODEL), BETA * D_MODEL ** -0.5),
        "ln_mix_g": near_one((DEPTH, D_MODEL)),
        "ln_mix_b": nrm((DEPTH, D_MODEL), 0.02),
        "ffn_w_gate": nrm((N_DENSE, D_MODEL, FF_DENSE), D_MODEL ** -0.5),
        "ffn_w_up": nrm((N_DENSE, D_MODEL, FF_DENSE), D_MODEL ** -0.5),
        "ffn_w_down": nrm((N_DENSE, FF_DENSE, D_MODEL), BETA * FF_DENSE ** -0.5),
        "moe_router": nrm((N_MOE, D_MODEL, N_EXPERTS), D_MODEL ** -0.5),
        "moe_w_gate": nrm((N_MOE, N_EXPERTS, D_MODEL, FF_EXPERT), D_MODEL ** -0.5),
        "moe_w_up": nrm((N_MOE, N_EXPERTS, D_MODEL, FF_EXPERT), D_MODEL ** -0.5),
        "moe_w_down": nrm((N_MOE, N_EXPERTS, FF_EXPERT, D_MODEL), BETA * FF_EXPERT ** -0.5),
        "ln_ffn_g": near_one((DEPTH, D_MODEL)),
        "ln_ffn_b": nrm((DEPTH, D_MODEL), 0.02),
    }


def reference(x, mix_w_in, conv_dw_w, conv_dw_b, conv_ln_g, conv_ln_b, conv_w_out,
              ssm_conv_w, ssm_conv_b, ssm_dt_bias, ssm_a_log, ssm_d, ssm_norm_w, ssm_w_out,
              mix_w_out, ln_mix_g, ln_mix_b, ffn_w_gate, ffn_w_up, ffn_w_down,
              moe_router, moe_w_gate, moe_w_up, moe_w_down, ln_ffn_g, ln_ffn_b):
    for i in range(DEPTH):
        mix = hybrid_mixer(x, mix_w_in[i], conv_dw_w[i], conv_dw_b[i], conv_ln_g[i], conv_ln_b[i],
                           conv_w_out[i], ssm_conv_w[i], ssm_conv_b[i], ssm_dt_bias[i],
                           ssm_a_log[i], ssm_d[i], ssm_norm_w[i], ssm_w_out[i], mix_w_out[i])
        x = layer_norm(ALPHA * x + mix, ln_mix_g[i], ln_mix_b[i])
        j = i // 2
        if i % 2 == 0:
            f = swiglu(x, ffn_w_gate[j], ffn_w_up[j], ffn_w_down[j])
        else:
            f = moe_swiglu(x, moe_router[j], moe_w_gate[j], moe_w_up[j], moe_w_down[j])
        x = layer_norm(ALPHA * x + f, ln_ffn_g[i], ln_ffn_b[i])
    return x
```

```python
import functools
import math

import jax
import jax.numpy as jnp
from jax import lax
from jax.experimental import pallas as pl
from jax.experimental.pallas import tpu as pltpu

F32 = jnp.float32
BF16 = jnp.bfloat16

D_MODEL = 1024
DEPTH = 2
CONV_K = 31
D_INNER = 2048
HEADDIM = 64
HEADS = D_INNER // HEADDIM
GROUPS = 4
HEADS_PER_GROUP = HEADS // GROUPS
STATE = 128
SSM_K = 4
CHUNK = 128
GROUP_W = D_INNER // GROUPS
BC_W = 2 * GROUPS * STATE
N_EXPERTS = 8
LN_EPS = 1e-5
RMS_EPS = 1e-5
ALPHA = (2 * DEPTH) ** 0.25

COL_Z = 2 * D_MODEL
COL_XS = COL_Z + D_INNER
COL_BC = COL_XS + D_INNER
COL_DT = COL_BC + BC_W
COL_GATES = COL_DT + HEADS
MAIN_W = COL_DT

LANES = 128
SUBLANES = 8
VMEM_LIMIT = 56 * 1024 * 1024


def _params(sem):
    return pltpu.CompilerParams(dimension_semantics=sem, vmem_limit_bytes=VMEM_LIMIT)


def _layer_norm(v, g, b):
    mu = jnp.mean(v, axis=-1, keepdims=True)
    d = v - mu
    var = jnp.mean(d * d, axis=-1, keepdims=True)
    return d * lax.rsqrt(var + LN_EPS) * g + b


def _silu(v):
    return v * jax.nn.sigmoid(v)


def _mm_body(x_ref, w_ref, o_ref):
    o_ref[...] = jnp.dot(x_ref[...], w_ref[...], preferred_element_type=F32).astype(o_ref.dtype)


def _matmul(x, w, *, n_blocks, tn, out_dtype, col_block0=0, tm=1024):
    t, k = x.shape
    tm = min(tm, t)
    return pl.pallas_call(
        _mm_body,
        grid=(n_blocks, t // tm),
        in_specs=[
            pl.BlockSpec((tm, k), lambda n, m: (m, 0)),
            pl.BlockSpec((k, tn), lambda n, m: (0, n + col_block0)),
        ],
        out_specs=pl.BlockSpec((tm, tn), lambda n, m: (m, n)),
        out_shape=jax.ShapeDtypeStruct((t, n_blocks * tn), out_dtype),
        compiler_params=_params(("arbitrary", "arbitrary")),
        name="proj_matmul",
    )(x, w)


CONV_TL = 256
CONV_HALO = 32
CONV_RB = 64


def _conv_body(cv_ref, cg_ref, dww_ref, dwb_ref, lng_ref, lnb_ref, wout_ref, o_ref, ubuf, cbuf):
    l = pl.program_id(1)

    @pl.when(l == 0)
    def _():
        ubuf[0:CONV_HALO, :] = jnp.zeros((CONV_HALO, D_MODEL), F32)

    @pl.when(l > 0)
    def _():
        ubuf[0:CONV_HALO, :] = ubuf[CONV_TL:CONV_TL + CONV_HALO, :]

    cv = cv_ref[...].astype(F32)
    cg = cg_ref[...].astype(F32)
    ubuf[CONV_HALO:CONV_HALO + CONV_TL, :] = cv * jax.nn.sigmoid(cg)

    base = CONV_HALO - (CONV_K - 1)
    for c in range(D_MODEL // LANES):
        cs = slice(c * LANES, (c + 1) * LANES)
        wc = dww_ref[:, cs]
        bc = dwb_ref[:, cs]
        for r in range(CONV_TL // CONV_RB):
            acc = jnp.broadcast_to(bc, (CONV_RB, LANES))
            for k in range(CONV_K):
                r0 = base + k + r * CONV_RB
                acc = acc + wc[k:k + 1, :] * ubuf[r0:r0 + CONV_RB, cs]
            cbuf[r * CONV_RB:(r + 1) * CONV_RB, cs] = acc

    v = _layer_norm(cbuf[...], lng_ref[...], lnb_ref[...])
    s = _silu(v).astype(BF16)
    o_ref[...] = jnp.dot(s, wout_ref[...], preferred_element_type=F32).astype(o_ref.dtype)


def _conv_branch(proj, dww, dwb, lng, lnb, wout, bsz, seqlen):
    nl = seqlen // CONV_TL
    row = lambda b, l: b * nl + l
    const = lambda b, l: (0, 0)
    return pl.pallas_call(
        _conv_body,
        grid=(bsz, nl),
        in_specs=[
            pl.BlockSpec((CONV_TL, D_MODEL), lambda b, l: (row(b, l), 0)),
            pl.BlockSpec((CONV_TL, D_MODEL), lambda b, l: (row(b, l), 1)),
            pl.BlockSpec((CONV_K, D_MODEL), const),
            pl.BlockSpec((1, D_MODEL), const),
            pl.BlockSpec((1, D_MODEL), const),
            pl.BlockSpec((1, D_MODEL), const),
            pl.BlockSpec((D_MODEL, D_MODEL), const),
        ],
        out_specs=pl.BlockSpec((CONV_TL, D_MODEL), lambda b, l: (row(b, l), 0)),
        out_shape=jax.ShapeDtypeStruct((bsz * seqlen, D_MODEL), BF16),
        scratch_shapes=[
            pltpu.VMEM((CONV_HALO + CONV_TL, D_MODEL), F32),
            pltpu.VMEM((CONV_TL, D_MODEL), F32),
        ],
        compiler_params=_params(("arbitrary", "arbitrary")),
        name="conv_branch",
    )(proj, proj, dww, dwb, lng, lnb, wout)


def _conv4_silu(cur_ref, halo_ref, w_ref, b_ref):
    cur = cur_ref[...].astype(F32)
    ext = jnp.concatenate([halo_ref[...], cur], axis=0)
    out = b_ref[...]
    for k in range(SSM_K):
        off = SUBLANES - (SSM_K - 1) + k
        out = out + w_ref[k:k + 1, :] * ext[off:off + CHUNK, :]
    halo_ref[...] = cur[CHUNK - SUBLANES:, :]
    return _silu(out)


def _ssd_body(xs_ref, bc_ref, z_ref, dt_ref, cwx_ref, cbx_ref, cwbc_ref, cbbc_ref, dtb_ref, alog_ref,
              dvec_ref, nw_ref, o_ref, halo_x, halo_bc, state, ybuf):
    c = pl.program_id(1)

    @pl.when(c == 0)
    def _():
        halo_x[...] = jnp.zeros_like(halo_x)
        halo_bc[...] = jnp.zeros_like(halo_bc)
        state[...] = jnp.zeros_like(state)

    xs = _conv4_silu(xs_ref, halo_x, cwx_ref, cbx_ref)
    bc = _conv4_silu(bc_ref, halo_bc, cwbc_ref, cbbc_ref)
    xs_b = xs.astype(BF16)

    dt_raw = dt_ref[...] + dtb_ref[...]
    dt = jnp.maximum(dt_raw, 0.0) + jnp.log1p(jnp.exp(-jnp.abs(dt_raw)))
    da = dt * (-jnp.exp(alog_ref[...]))
    row = lax.broadcasted_iota(jnp.int32, (CHUNK, CHUNK), 0)
    col = lax.broadcasted_iota(jnp.int32, (CHUNK, CHUNK), 1)
    causal = row >= col
    acum = jnp.dot(causal.astype(F32), da, preferred_element_type=F32,
                   precision=lax.Precision.HIGHEST)
    acum_t = acum.T
    dt_t = dt.T
    last = acum_t[:, CHUNK - 1:CHUNK]
    wt_t = jnp.exp(last - acum_t) * dt_t
    chunk_decay = jnp.exp(acum[CHUNK - 1:CHUNK, :])

    for g in range(GROUPS):
        bg = bc[:, g * STATE:(g + 1) * STATE]
        cg = bc[:, GROUPS * STATE + g * STATE:GROUPS * STATE + (g + 1) * STATE]
        bg_b = bg.astype(BF16)
        cb = lax.dot_general(cg.astype(BF16), bg_b, (((1,), (1,)), ((), ())),
                             preferred_element_type=F32)
        bg_t = bg.T
        for j in range(HEADS_PER_GROUP):
            h = g * HEADS_PER_GROUP + j
            hs = slice(h * HEADDIM, (h + 1) * HEADDIM)
            colb = jnp.broadcast_to(acum[:, h:h + 1], (CHUNK, CHUNK))
            rowb = jnp.broadcast_to(acum_t[h:h + 1, :], (CHUNK, CHUNK))
            decay = jnp.exp(jnp.where(causal, colb - rowb, -jnp.inf))
            m = (cb * decay * dt_t[h:h + 1, :]).astype(BF16)
            x_h = xs_b[:, hs]
            prev = state[h]
            y_h = jnp.dot(m, x_h, preferred_element_type=F32)
            y_h = y_h + jnp.dot((cg * jnp.exp(colb)).astype(BF16), prev.astype(BF16),
                                preferred_element_type=F32)
            s_new = jnp.dot((bg_t * wt_t[h:h + 1, :]).astype(BF16), x_h, preferred_element_type=F32)
            state[h] = prev * chunk_decay[:, h:h + 1] + s_new
            ybuf[:, hs] = y_h

    y = ybuf[...] + dvec_ref[...] * xs
    hgate = y * _silu(z_ref[...].astype(F32))
    for g in range(GROUPS):
        gs = slice(g * GROUP_W, (g + 1) * GROUP_W)
        hg = hgate[:, gs]
        ms = jnp.mean(hg * hg, axis=-1, keepdims=True)
        o_ref[:, gs] = (hg * lax.rsqrt(ms + RMS_EPS) * nw_ref[:, gs]).astype(o_ref.dtype)


def _ssd_branch(proj, dt_raw, cw, cb, dtb, alog, dvec, nw, bsz, seqlen):
    nc = seqlen // CHUNK
    row = lambda b, c: b * nc + c
    const = lambda b, c: (0, 0)
    xs_blk = COL_XS // D_INNER
    bc_blk = COL_BC // BC_W
    z_blk = COL_Z // D_INNER
    return pl.pallas_call(
        _ssd_body,
        grid=(bsz, nc),
        in_specs=[
            pl.BlockSpec((CHUNK, D_INNER), lambda b, c: (row(b, c), xs_blk)),
            pl.BlockSpec((CHUNK, BC_W), lambda b, c: (row(b, c), bc_blk)),
            pl.BlockSpec((CHUNK, D_INNER), lambda b, c: (row(b, c), z_blk)),
            pl.BlockSpec((CHUNK, LANES), lambda b, c: (row(b, c), 0)),
            pl.BlockSpec((SSM_K, D_INNER), lambda b, c: (0, 0)),
            pl.BlockSpec((1, D_INNER), lambda b, c: (0, 0)),
            pl.BlockSpec((SSM_K, BC_W), lambda b, c: (0, D_INNER // BC_W)),
            pl.BlockSpec((1, BC_W), lambda b, c: (0, D_INNER // BC_W)),
            pl.BlockSpec((1, LANES), const),
            pl.BlockSpec((1, LANES), const),
            pl.BlockSpec((1, D_INNER), const),
            pl.BlockSpec((1, D_INNER), const),
        ],
        out_specs=pl.BlockSpec((CHUNK, D_INNER), lambda b, c: (row(b, c), 0)),
        out_shape=jax.ShapeDtypeStruct((bsz * seqlen, D_INNER), BF16),
        scratch_shapes=[
            pltpu.VMEM((SUBLANES, D_INNER), F32),
            pltpu.VMEM((SUBLANES, BC_W), F32),
            pltpu.VMEM((HEADS, STATE, HEADDIM), F32),
            pltpu.VMEM((CHUNK, D_INNER), F32),
        ],
        compiler_params=_params(("arbitrary", "arbitrary")),
        name="ssd_branch",
    )(proj, proj, proj, dt_raw, cw, cb, cw, cb, dtb, alog, dvec, nw)


def _merge_body(x_ref, yc_ref, yn_ref, gt_ref, wssm_ref, wout_ref, g_ref, b_ref, o_ref, ob_ref):
    y_ssm = jnp.dot(yn_ref[...], wssm_ref[...], preferred_element_type=F32)
    gates = gt_ref[...].astype(F32)
    hmix = (jax.nn.sigmoid(gates[:, :D_MODEL]) * yc_ref[...].astype(F32)
            + jax.nn.sigmoid(gates[:, D_MODEL:]) * y_ssm)
    mix = jnp.dot(hmix.astype(BF16), wout_ref[...], preferred_element_type=F32)
    out = _layer_norm(ALPHA * x_ref[...] + mix, g_ref[...], b_ref[...])
    o_ref[...] = out
    ob_ref[...] = out.astype(BF16)


def _merge(x, y_conv, y_n, gates, wssm, wout, g, b, tm=512):
    t = x.shape[0]
    tm = min(tm, t)
    rowb = lambda w: pl.BlockSpec((tm, w), lambda m: (m, 0))
    full = lambda a: pl.BlockSpec(a.shape, lambda m: (0, 0))
    return pl.pallas_call(
        _merge_body,
        grid=(t // tm,),
        in_specs=[rowb(D_MODEL), rowb(D_MODEL), rowb(D_INNER), rowb(2 * D_MODEL),
                  full(wssm), full(wout), full(g), full(b)],
        out_specs=[rowb(D_MODEL), rowb(D_MODEL)],
        out_shape=[jax.ShapeDtypeStruct((t, D_MODEL), F32), jax.ShapeDtypeStruct((t, D_MODEL), BF16)],
        compiler_params=_params(("arbitrary",)),
        name="merge_ln",
    )(x, y_conv, y_n, gates, wssm, wout, g, b)


def _ffn_body(x_ref, xb_ref, wg_ref, wu_ref, wd_ref, g_ref, b_ref, o_ref, ob_ref):
    xb = xb_ref[...]
    hg = jnp.dot(xb, wg_ref[...], preferred_element_type=F32)
    hu = jnp.dot(xb, wu_ref[...], preferred_element_type=F32)
    f = jnp.dot((_silu(hg) * hu).astype(BF16), wd_ref[...], preferred_element_type=F32)
    out = _layer_norm(ALPHA * x_ref[...] + f, g_ref[...], b_ref[...])
    o_ref[...] = out
    ob_ref[...] = out.astype(BF16)


def _ffn(x, xb, wg, wu, wd, g, b, tm=512):
    t = x.shape[0]
    tm = min(tm, t)
    rowb = lambda w: pl.BlockSpec((tm, w), lambda m: (m, 0))
    full = lambda a: pl.BlockSpec(a.shape, lambda m: (0, 0), pipeline_mode=pl.Buffered(1))
    return pl.pallas_call(
        _ffn_body,
        grid=(t // tm,),
        in_specs=[rowb(D_MODEL), rowb(D_MODEL), full(wg), full(wu), full(wd), full(g), full(b)],
        out_specs=[rowb(D_MODEL), rowb(D_MODEL)],
        out_shape=[jax.ShapeDtypeStruct((t, D_MODEL), F32), jax.ShapeDtypeStruct((t, D_MODEL), BF16)],
        compiler_params=_params(("arbitrary",)),
        name="ffn_ln",
    )(x, xb, wg, wu, wd, g, b)


def _router_body(x_ref, wr_ref, o_ref):
    logits = jnp.dot(x_ref[...], wr_ref[...], preferred_element_type=F32, precision=lax.Precision.HIGHEST)
    lane = lax.broadcasted_iota(jnp.int32, logits.shape, 1)
    logits = jnp.where(lane < N_EXPERTS, logits, -jnp.inf)
    m1 = jnp.max(logits, axis=-1, keepdims=True)
    i1 = jnp.min(jnp.where(logits == m1, lane, LANES), axis=-1, keepdims=True)
    rest = jnp.where(lane == i1, -jnp.inf, logits)
    m2 = jnp.max(rest, axis=-1, keepdims=True)
    i2 = jnp.min(jnp.where(rest == m2, lane, LANES), axis=-1, keepdims=True)
    e2 = jnp.exp(m2 - m1)
    denom = 1.0 + e2
    o_ref[...] = jnp.where(lane == i1, 1.0 / denom, jnp.where(lane == i2, e2 / denom, 0.0))


def _router(x, wr_pad, tm=1024):
    t = x.shape[0]
    tm = min(tm, t)
    return pl.pallas_call(
        _router_body,
        grid=(t // tm,),
        in_specs=[pl.BlockSpec((tm, D_MODEL), lambda m: (m, 0)),
                  pl.BlockSpec((D_MODEL, LANES), lambda m: (0, 0))],
        out_specs=pl.BlockSpec((tm, LANES), lambda m: (m, 0)),
        out_shape=jax.ShapeDtypeStruct((t, LANES), F32),
        compiler_params=_params(("arbitrary",)),
        name="router",
    )(x, wr_pad)


MOE_FF_SPLIT = 2


def _moe_body(x_ref, xb_ref, comb_ref, wg_ref, wu_ref, wd_ref, g_ref, b_ref, o_ref, acc):
    e = pl.program_id(1)
    f = pl.program_id(2)
    first = jnp.logical_and(e == 0, f == 0)
    last = jnp.logical_and(e == N_EXPERTS - 1, f == MOE_FF_SPLIT - 1)

    @pl.when(first)
    def _():
        acc[...] = jnp.zeros_like(acc)

    xb = xb_ref[...]
    hg = jnp.dot(xb, wg_ref[0], preferred_element_type=F32)
    hu = jnp.dot(xb, wu_ref[0], preferred_element_type=F32)
    y = jnp.dot((_silu(hg) * hu).astype(BF16), wd_ref[0], preferred_element_type=F32)
    lane = lax.broadcasted_iota(jnp.int32, comb_ref.shape, 1)
    w = jnp.sum(jnp.where(lane == e, comb_ref[...], 0.0), axis=-1, keepdims=True)
    acc[...] += w * y

    @pl.when(last)
    def _():
        o_ref[...] = _layer_norm(ALPHA * x_ref[...] + acc[...], g_ref[...], b_ref[...])


def _moe(x, xb, comb, wg, wu, wd, g, b, tm=512):
    t = x.shape[0]
    tm = min(tm, t)
    ffs = wg.shape[-1] // MOE_FF_SPLIT
    rowb = lambda w: pl.BlockSpec((tm, w), lambda m, e, f: (m, 0))
    full = lambda a: pl.BlockSpec(a.shape, lambda m, e, f: (0, 0))
    return pl.pallas_call(
        _moe_body,
        grid=(t // tm, N_EXPERTS, MOE_FF_SPLIT),
        in_specs=[rowb(D_MODEL), rowb(D_MODEL), rowb(LANES),
                  pl.BlockSpec((1, D_MODEL, ffs), lambda m, e, f: (e, 0, f)),
                  pl.BlockSpec((1, D_MODEL, ffs), lambda m, e, f: (e, 0, f)),
                  pl.BlockSpec((1, ffs, D_MODEL), lambda m, e, f: (e, f, 0)),
                  full(g), full(b)],
        out_specs=rowb(D_MODEL),
        out_shape=jax.ShapeDtypeStruct((t, D_MODEL), F32),
        scratch_shapes=[pltpu.VMEM((tm, D_MODEL), F32)],
        compiler_params=_params(("arbitrary", "arbitrary", "arbitrary")),
        name="moe_ln",
    )(x, xb, comb, wg, wu, wd, g, b)


def _pad_lanes(v):
    return jnp.pad(v, ((0, 0), (0, LANES - v.shape[-1])))


def _mixer(x, xb, bsz, seqlen, w_in, conv_dw_w, conv_dw_b, conv_ln_g, conv_ln_b, conv_w_out,
           ssm_conv_w, ssm_conv_b, ssm_dt_bias, ssm_a_log, ssm_d, ssm_norm_w, ssm_w_out, w_out, ln_g, ln_b):
    w_in_b = w_in.astype(BF16)
    proj = _matmul(xb, w_in_b, n_blocks=MAIN_W // 1024, tn=1024, out_dtype=BF16)
    dt_raw = _matmul(xb, _pad_lanes(w_in_b[:, COL_DT:COL_GATES]), n_blocks=1, tn=LANES, out_dtype=F32)
    gates = _matmul(xb, w_in_b[:, COL_GATES:], n_blocks=2, tn=1024, out_dtype=BF16)
    y_conv = _conv_branch(proj, conv_dw_w, conv_dw_b[None], conv_ln_g[None], conv_ln_b[None],
                          conv_w_out.astype(BF16), bsz, seqlen)
    y_n = _ssd_branch(proj, dt_raw, ssm_conv_w, ssm_conv_b[None], _pad_lanes(ssm_dt_bias[None]),
                      _pad_lanes(ssm_a_log[None]), jnp.repeat(ssm_d, HEADDIM)[None], ssm_norm_w[None],
                      bsz, seqlen)
    return _merge(x, y_conv, y_n, gates, ssm_w_out.astype(BF16), w_out.astype(BF16), ln_g[None], ln_b[None])


def kernel(x, mix_w_in, conv_dw_w, conv_dw_b, conv_ln_g, conv_ln_b, conv_w_out, ssm_conv_w, ssm_conv_b,
           ssm_dt_bias, ssm_a_log, ssm_d, ssm_norm_w, ssm_w_out, mix_w_out, ln_mix_g, ln_mix_b,
           ffn_w_gate, ffn_w_up, ffn_w_down, moe_router, moe_w_gate, moe_w_up, moe_w_down,
           ln_ffn_g, ln_ffn_b):
    bsz, seqlen, d = x.shape
    xf = x.reshape(bsz * seqlen, d)
    xb = xf.astype(BF16)
    for i in range(DEPTH):
        xf, xb = _mixer(xf, xb, bsz, seqlen, mix_w_in[i], conv_dw_w[i], conv_dw_b[i], conv_ln_g[i],
                        conv_ln_b[i], conv_w_out[i], ssm_conv_w[i], ssm_conv_b[i], ssm_dt_bias[i],
                        ssm_a_log[i], ssm_d[i], ssm_norm_w[i], ssm_w_out[i], mix_w_out[i],
                        ln_mix_g[i], ln_mix_b[i])
        j = i // 2
        if i % 2 == 0:
            xf, xb = _ffn(xf, xb, ffn_w_gate[j].astype(BF16), ffn_w_up[j].astype(BF16),
                          ffn_w_down[j].astype(BF16), ln_ffn_g[i][None], ln_ffn_b[i][None])
        else:
            comb = _router(xf, _pad_lanes(moe_router[j]))
            xf = _moe(xf, xb, comb, moe_w_gate[j].astype(BF16), moe_w_up[j].astype(BF16),
                      moe_w_down[j].astype(BF16), ln_ffn_g[i][None], ln_ffn_b[i][None])
            xb = xf.astype(BF16)
    return xf.reshape(bsz, seqlen, d)
```

```python
import functools
import math

import jax
import jax.numpy as jnp
from jax import lax
from jax.experimental import pallas as pl
from jax.experimental.pallas import tpu as pltpu

F32 = jnp.float32
BF16 = jnp.bfloat16

D_MODEL = 1024
DEPTH = 2
CONV_K = 31
D_INNER = 2048
HEADDIM = 64
HEADS = D_INNER // HEADDIM
GROUPS = 4
HEADS_PER_GROUP = HEADS // GROUPS
STATE = 128
SSM_K = 4
CHUNK = 128
GROUP_W = D_INNER // GROUPS
BC_W = 2 * GROUPS * STATE
N_EXPERTS = 8
LN_EPS = 1e-5
RMS_EPS = 1e-5
ALPHA = (2 * DEPTH) ** 0.25

COL_Z = 2 * D_MODEL
COL_XS = COL_Z + D_INNER
COL_BC = COL_XS + D_INNER
COL_DT = COL_BC + BC_W
COL_GATES = COL_DT + HEADS
MAIN_W = COL_DT

LANES = 128
SUBLANES = 8
VMEM_LIMIT = 56 * 1024 * 1024


def _params(sem):
    return pltpu.CompilerParams(dimension_semantics=sem, vmem_limit_bytes=VMEM_LIMIT)


def _layer_norm(v, g, b):
    mu = jnp.mean(v, axis=-1, keepdims=True)
    d = v - mu
    var = jnp.mean(d * d, axis=-1, keepdims=True)
    return d * lax.rsqrt(var + LN_EPS) * g + b


def _silu(v):
    return v * jax.nn.sigmoid(v)


def _mm_body(x_ref, w_ref, o_ref):
    o_ref[...] = jnp.dot(x_ref[...], w_ref[...], preferred_element_type=F32).astype(o_ref.dtype)


def _matmul(x, w, *, n_blocks, tn, out_dtype, col_block0=0, tm=1024):
    t, k = x.shape
    tm = min(tm, t)
    return pl.pallas_call(
        _mm_body,
        grid=(n_blocks, t // tm),
        in_specs=[
            pl.BlockSpec((tm, k), lambda n, m: (m, 0)),
            pl.BlockSpec((k, tn), lambda n, m: (0, n + col_block0)),
        ],
        out_specs=pl.BlockSpec((tm, tn), lambda n, m: (m, n)),
        out_shape=jax.ShapeDtypeStruct((t, n_blocks * tn), out_dtype),
        compiler_params=_params(("arbitrary", "arbitrary")),
        name="proj_matmul",
    )(x, w)


CONV_TL = 256
CONV_HALO = 32
CONV_RB = 64


def _conv_body(cv_ref, cg_ref, dww_ref, dwb_ref, lng_ref, lnb_ref, wout_ref, o_ref, ubuf, cbuf):
    l = pl.program_id(1)

    @pl.when(l == 0)
    def _():
        ubuf[0:CONV_HALO, :] = jnp.zeros((CONV_HALO, D_MODEL), F32)

    @pl.when(l > 0)
    def _():
        ubuf[0:CONV_HALO, :] = ubuf[CONV_TL:CONV_TL + CONV_HALO, :]

    cv = cv_ref[...].astype(F32)
    cg = cg_ref[...].astype(F32)
    ubuf[CONV_HALO:CONV_HALO + CONV_TL, :] = cv * jax.nn.sigmoid(cg)

    base = CONV_HALO - (CONV_K - 1)
    for c in range(D_MODEL // LANES):
        cs = slice(c * LANES, (c + 1) * LANES)
        wc = dww_ref[:, cs]
        bc = dwb_ref[:, cs]
        for r in range(CONV_TL // CONV_RB):
            acc = jnp.broadcast_to(bc, (CONV_RB, LANES))
            for k in range(CONV_K):
                r0 = base + k + r * CONV_RB
                acc = acc + wc[k:k + 1, :] * ubuf[r0:r0 + CONV_RB, cs]
            cbuf[r * CONV_RB:(r + 1) * CONV_RB, cs] = acc

    v = _layer_norm(cbuf[...], lng_ref[...], lnb_ref[...])
    s = _silu(v).astype(BF16)
    o_ref[...] = jnp.dot(s, wout_ref[...], preferred_element_type=F32).astype(o_ref.dtype)


def _conv_branch(proj, dww, dwb, lng, lnb, wout, bsz, seqlen):
    nl = seqlen // CONV_TL
    row = lambda b, l: b * nl + l
    const = lambda b, l: (0, 0)
    return pl.pallas_call(
        _conv_body,
        grid=(bsz, nl),
        in_specs=[
            pl.BlockSpec((CONV_TL, D_MODEL), lambda b, l: (row(b, l), 0)),
            pl.BlockSpec((CONV_TL, D_MODEL), lambda b, l: (row(b, l), 1)),
            pl.BlockSpec((CONV_K, D_MODEL), const),
            pl.BlockSpec((1, D_MODEL), const),
            pl.BlockSpec((1, D_MODEL), const),
            pl.BlockSpec((1, D_MODEL), const),
            pl.BlockSpec((D_MODEL, D_MODEL), const),
        ],
        out_specs=pl.BlockSpec((CONV_TL, D_MODEL), lambda b, l: (row(b, l), 0)),
        out_shape=jax.ShapeDtypeStruct((bsz * seqlen, D_MODEL), BF16),
        scratch_shapes=[
            pltpu.VMEM((CONV_HALO + CONV_TL, D_MODEL), F32),
            pltpu.VMEM((CONV_TL, D_MODEL), F32),
        ],
        compiler_params=_params(("arbitrary", "arbitrary")),
        name="conv_branch",
    )(proj, proj, dww, dwb, lng, lnb, wout)


def _conv4_silu(cur_ref, halo_ref, w_ref, b_ref):
    cur = cur_ref[...].astype(F32)
    ext = jnp.concatenate([halo_ref[...], cur], axis=0)
    out = b_ref[...]
    for k in range(SSM_K):
        off = SUBLANES - (SSM_K - 1) + k
        out = out + w_ref[k:k + 1, :] * ext[off:off + CHUNK, :]
    halo_ref[...] = cur[CHUNK - SUBLANES:, :]
    return _silu(out)


def _ssd_body(xs_ref, bc_ref, z_ref, dt_ref, cwx_ref, cbx_ref, cwbc_ref, cbbc_ref, dtb_ref, alog_ref,
              dvec_ref, nw_ref, o_ref, halo_x, halo_bc, state, ybuf):
    c = pl.program_id(1)

    @pl.when(c == 0)
    def _():
        halo_x[...] = jnp.zeros_like(halo_x)
        halo_bc[...] = jnp.zeros_like(halo_bc)
        state[...] = jnp.zeros_like(state)

    xs = _conv4_silu(xs_ref, halo_x, cwx_ref, cbx_ref)
    bc = _conv4_silu(bc_ref, halo_bc, cwbc_ref, cbbc_ref)
    xs_b = xs.astype(BF16)

    dt_raw = dt_ref[...] + dtb_ref[...]
    dt = jnp.maximum(dt_raw, 0.0) + jnp.log1p(jnp.exp(-jnp.abs(dt_raw)))
    da = dt * (-jnp.exp(alog_ref[...]))
    row = lax.broadcasted_iota(jnp.int32, (CHUNK, CHUNK), 0)
    col = lax.broadcasted_iota(jnp.int32, (CHUNK, CHUNK), 1)
    causal = row >= col
    acum = jnp.dot(causal.astype(F32), da, preferred_element_type=F32,
                   precision=lax.Precision.HIGHEST)
    acum_t = acum.T
    dt_t = dt.T
    last = acum_t[:, CHUNK - 1:CHUNK]
    wt_t = jnp.exp(last - acum_t) * dt_t
    chunk_decay = jnp.exp(acum[CHUNK - 1:CHUNK, :])

    for g in range(GROUPS):
        bg = bc[:, g * STATE:(g + 1) * STATE]
        cg = bc[:, GROUPS * STATE + g * STATE:GROUPS * STATE + (g + 1) * STATE]
        bg_b = bg.astype(BF16)
        cb = lax.dot_general(cg.astype(BF16), bg_b, (((1,), (1,)), ((), ())),
                             preferred_element_type=F32)
        bg_t = bg.T
        for j in range(HEADS_PER_GROUP):
            h = g * HEADS_PER_GROUP + j
            hs = slice(h * HEADDIM, (h + 1) * HEADDIM)
            colb = jnp.broadcast_to(acum[:, h:h + 1], (CHUNK, CHUNK))
            rowb = jnp.broadcast_to(acum_t[h:h + 1, :], (CHUNK, CHUNK))
            decay = jnp.exp(jnp.where(causal, colb - rowb, -jnp.inf))
            m = (cb * decay * dt_t[h:h + 1, :]).astype(BF16)
            x_h = xs_b[:, hs]
            prev = state[h]
            y_h = jnp.dot(m, x_h, preferred_element_type=F32)
            y_h = y_h + jnp.dot((cg * jnp.exp(colb)).astype(BF16), prev.astype(BF16),
                                preferred_element_type=F32)
            s_new = jnp.dot((bg_t * wt_t[h:h + 1, :]).astype(BF16), x_h, preferred_element_type=F32)
            state[h] = prev * chunk_decay[:, h:h + 1] + s_new
            ybuf[:, hs] = y_h

    y = ybuf[...] + dvec_ref[...] * xs
    hgate = y * _silu(z_ref[...].astype(F32))
    for g in range(GROUPS):
        gs = slice(g * GROUP_W, (g + 1) * GROUP_W)
        hg = hgate[:, gs]
        ms = jnp.mean(hg * hg, axis=-1, keepdims=True)
        o_ref[:, gs] = (hg * lax.rsqrt(ms + RMS_EPS) * nw_ref[:, gs]).astype(o_ref.dtype)


def _ssd_branch(proj, dt_raw, cw, cb, dtb, alog, dvec, nw, bsz, seqlen):
    nc = seqlen // CHUNK
    row = lambda b, c: b * nc + c
    const = lambda b, c: (0, 0)
    xs_blk = COL_XS // D_INNER
    bc_blk = COL_BC // BC_W
    z_blk = COL_Z // D_INNER
    return pl.pallas_call(
        _ssd_body,
        grid=(bsz, nc),
        in_specs=[
            pl.BlockSpec((CHUNK, D_INNER), lambda b, c: (row(b, c), xs_blk)),
            pl.BlockSpec((CHUNK, BC_W), lambda b, c: (row(b, c), bc_blk)),
            pl.BlockSpec((CHUNK, D_INNER), lambda b, c: (row(b, c), z_blk)),
            pl.BlockSpec((CHUNK, LANES), lambda b, c: (row(b, c), 0)),
            pl.BlockSpec((SSM_K, D_INNER), lambda b, c: (0, 0)),
            pl.BlockSpec((1, D_INNER), lambda b, c: (0, 0)),
            pl.BlockSpec((SSM_K, BC_W), lambda b, c: (0, D_INNER // BC_W)),
            pl.BlockSpec((1, BC_W), lambda b, c: (0, D_INNER // BC_W)),
            pl.BlockSpec((1, LANES), const),
            pl.BlockSpec((1, LANES), const),
            pl.BlockSpec((1, D_INNER), const),
            pl.BlockSpec((1, D_INNER), const),
        ],
        out_specs=pl.BlockSpec((CHUNK, D_INNER), lambda b, c: (row(b, c), 0)),
        out_shape=jax.ShapeDtypeStruct((bsz * seqlen, D_INNER), BF16),
        scratch_shapes=[
            pltpu.VMEM((SUBLANES, D_INNER), F32),
            pltpu.VMEM((SUBLANES, BC_W), F32),
            pltpu.VMEM((HEADS, STATE, HEADDIM), F32),
            pltpu.VMEM((CHUNK, D_INNER), F32),
        ],
        compiler_params=_params(("arbitrary", "arbitrary")),
        name="ssd_branch",
    )(proj, proj, proj, dt_raw, cw, cb, cw, cb, dtb, alog, dvec, nw)


def _merge_body(x_ref, yc_ref, yn_ref, gt_ref, wssm_ref, wout_ref, g_ref, b_ref, o_ref, ob_ref):
    y_ssm = jnp.dot(yn_ref[...], wssm_ref[...], preferred_element_type=F32)
    gates = gt_ref[...].astype(F32)
    hmix = (jax.nn.sigmoid(gates[:, :D_MODEL]) * yc_ref[...].astype(F32)
            + jax.nn.sigmoid(gates[:, D_MODEL:]) * y_ssm)
    mix = jnp.dot(hmix.astype(BF16), wout_ref[...], preferred_element_type=F32)
    out = _layer_norm(ALPHA * x_ref[...] + mix, g_ref[...], b_ref[...])
    o_ref[...] = out
    ob_ref[...] = out.astype(BF16)


def _merge(x, y_conv, y_n, gates, wssm, wout, g, b, tm=512):
    t = x.shape[0]
    tm = min(tm, t)
    rowb = lambda w: pl.BlockSpec((tm, w), lambda m: (m, 0))
    full = lambda a: pl.BlockSpec(a.shape, lambda m: (0, 0))
    return pl.pallas_call(
        _merge_body,
        grid=(t // tm,),
        in_specs=[rowb(D_MODEL), rowb(D_MODEL), rowb(D_INNER), rowb(2 * D_MODEL),
                  full(wssm), full(wout), full(g), full(b)],
        out_specs=[rowb(D_MODEL), rowb(D_MODEL)],
        out_shape=[jax.ShapeDtypeStruct((t, D_MODEL), F32), jax.ShapeDtypeStruct((t, D_MODEL), BF16)],
        compiler_params=_params(("arbitrary",)),
        name="merge_ln",
    )(x, y_conv, y_n, gates, wssm, wout, g, b)


def _ffn_body(x_ref, xb_ref, wg_ref, wu_ref, wd_ref, g_ref, b_ref, o_ref, ob_ref):
    xb = xb_ref[...]
    hg = jnp.dot(xb, wg_ref[...], preferred_element_type=F32)
    hu = jnp.dot(xb, wu_ref[...], preferred_element_type=F32)
    f = jnp.dot((_silu(hg) * hu).astype(BF16), wd_ref[...], preferred_element_type=F32)
    out = _layer_norm(ALPHA * x_ref[...] + f, g_ref[...], b_ref[...])
    o_ref[...] = out
    ob_ref[...] = out.astype(BF16)


def _ffn(x, xb, wg, wu, wd, g, b, tm=512):
    t = x.shape[0]
    tm = min(tm, t)
    rowb = lambda w: pl.BlockSpec((tm, w), lambda m: (m, 0))
    full = lambda a: pl.BlockSpec(a.shape, lambda m: (0, 0), pipeline_mode=pl.Buffered(1))
    return pl.pallas_call(
        _ffn_body,
        grid=(t // tm,),
        in_specs=[rowb(D_MODEL), rowb(D_MODEL), full(wg), full(wu), full(wd), full(g), full(b)],
        out_specs=[rowb(D_MODEL), rowb(D_MODEL)],
        out_shape=[jax.ShapeDtypeStruct((t, D_MODEL), F32), jax.ShapeDtypeStruct((t, D_MODEL), BF16)],
        compiler_params=_params(("arbitrary",)),
        name="ffn_ln",
    )(x, xb, wg, wu, wd, g, b)


def _router_body(x_ref, wr_ref, wts_ref, meta_ref, cnt_ref, carry):
    @pl.when(pl.program_id(0) == 0)
    def _():
        carry[...] = jnp.zeros_like(carry)

    logits = jnp.dot(x_ref[...], wr_ref[...], preferred_element_type=F32, precision=lax.Precision.HIGHEST)
    tm = logits.shape[0]
    lane = lax.broadcasted_iota(jnp.int32, logits.shape, 1)
    logits = jnp.where(lane < N_EXPERTS, logits, -jnp.inf)
    m1 = jnp.max(logits, axis=-1, keepdims=True)
    i1 = jnp.min(jnp.where(logits == m1, lane, LANES), axis=-1, keepdims=True)
    rest = jnp.where(lane == i1, -jnp.inf, logits)
    m2 = jnp.max(rest, axis=-1, keepdims=True)
    i2 = jnp.min(jnp.where(rest == m2, lane, LANES), axis=-1, keepdims=True)
    e2 = jnp.exp(m2 - m1)
    denom = 1.0 + e2
    wts_ref[...] = jnp.where(lane == 0, 1.0 / denom, jnp.where(lane == 1, e2 / denom, 0.0))

    chosen = jnp.logical_or(lane == i1, lane == i2)
    chosen_f = jnp.where(chosen, 1.0, 0.0)
    r = lax.broadcasted_iota(jnp.int32, (tm, tm), 0)
    c = lax.broadcasted_iota(jnp.int32, (tm, tm), 1)
    earlier = jnp.where(c < r, 1.0, 0.0).astype(BF16)
    prefix = jnp.dot(earlier, chosen_f.astype(BF16), preferred_element_type=F32) + carry[...]
    rank1 = jnp.sum(jnp.where(lane == i1, prefix, 0.0), axis=-1, keepdims=True).astype(jnp.int32)
    rank2 = jnp.sum(jnp.where(lane == i2, prefix, 0.0), axis=-1, keepdims=True).astype(jnp.int32)
    meta_ref[...] = jnp.where(lane == 0, i1, jnp.where(lane == 1, i2,
                              jnp.where(lane == 2, rank1, jnp.where(lane == 3, rank2, 0))))
    total = carry[...] + jnp.sum(chosen_f, axis=0, keepdims=True)
    carry[...] = total
    cnt_ref[...] = jnp.broadcast_to(total, cnt_ref.shape).astype(jnp.int32)


def _router(x, wr_pad, tm=1024):
    t = x.shape[0]
    tm = min(tm, t)
    return pl.pallas_call(
        _router_body,
        grid=(t // tm,),
        in_specs=[pl.BlockSpec((tm, D_MODEL), lambda m: (m, 0)),
                  pl.BlockSpec((D_MODEL, LANES), lambda m: (0, 0))],
        out_specs=[pl.BlockSpec((tm, LANES), lambda m: (m, 0)),
                   pl.BlockSpec((tm, LANES), lambda m: (m, 0)),
                   pl.BlockSpec((SUBLANES, LANES), lambda m: (0, 0))],
        out_shape=[jax.ShapeDtypeStruct((t, LANES), F32),
                   jax.ShapeDtypeStruct((t, LANES), jnp.int32),
                   jax.ShapeDtypeStruct((SUBLANES, LANES), jnp.int32)],
        scratch_shapes=[pltpu.VMEM((1, LANES), F32)],
        compiler_params=_params(("arbitrary",)),
        name="router",
    )(x, wr_pad)


MOE_FF_SPLIT = 2


MOE_TM = 512
MOE_DMA_ROWS = 2048


def _row_copy(src_hbm, dst_hbm, src_row, dst_row, sem):
    return pltpu.make_async_copy(src_hbm.at[pl.ds(src_row, 1)], dst_hbm.at[pl.ds(dst_row, 1)], sem)


def _dispatch_body(slot1_ref, slot2_ref, x_hbm, xs_in_hbm, xs_hbm, sem):
    del xs_in_hbm
    n = slot1_ref.shape[0]
    base = pl.program_id(0) * n

    def issue(t, carry):
        _row_copy(x_hbm, xs_hbm, base + t, slot1_ref[t], sem).start()
        _row_copy(x_hbm, xs_hbm, base + t, slot2_ref[t], sem).start()
        return carry

    lax.fori_loop(0, n, issue, 0, unroll=8)

    def drain(t, carry):
        _row_copy(x_hbm, xs_hbm, 0, 0, sem).wait()
        _row_copy(x_hbm, xs_hbm, 0, 0, sem).wait()
        return carry

    lax.fori_loop(0, n, drain, 0, unroll=8)


def _dispatch(x, slot1, slot2, xs_zero):
    t = x.shape[0]
    n = min(MOE_DMA_ROWS, t)
    smem = lambda: pl.BlockSpec((n,), lambda i: (i,), memory_space=pltpu.SMEM)
    return pl.pallas_call(
        _dispatch_body,
        grid=(t // n,),
        in_specs=[smem(), smem(), pl.BlockSpec(memory_space=pl.ANY), pl.BlockSpec(memory_space=pl.ANY)],
        out_specs=pl.BlockSpec(memory_space=pl.ANY),
        out_shape=jax.ShapeDtypeStruct(xs_zero.shape, xs_zero.dtype),
        scratch_shapes=[pltpu.SemaphoreType.DMA(())],
        input_output_aliases={3: 0},
        compiler_params=_params(("arbitrary",)),
        name="moe_dispatch",
    )(slot1, slot2, x, xs_zero)


def _gather_back_body(slot1_ref, slot2_ref, y_hbm, y1_hbm, y2_hbm, sem):
    n = slot1_ref.shape[0]
    base = pl.program_id(0) * n

    def issue(t, carry):
        _row_copy(y_hbm, y1_hbm, slot1_ref[t], base + t, sem).start()
        _row_copy(y_hbm, y2_hbm, slot2_ref[t], base + t, sem).start()
        return carry

    lax.fori_loop(0, n, issue, 0, unroll=8)

    def drain(t, carry):
        _row_copy(y_hbm, y1_hbm, 0, 0, sem).wait()
        _row_copy(y_hbm, y2_hbm, 0, 0, sem).wait()
        return carry

    lax.fori_loop(0, n, drain, 0, unroll=8)


def _gather_back(y, slot1, slot2, t):
    n = min(MOE_DMA_ROWS, t)
    smem = lambda: pl.BlockSpec((n,), lambda i: (i,), memory_space=pltpu.SMEM)
    out = jax.ShapeDtypeStruct((t, D_MODEL), F32)
    return pl.pallas_call(
        _gather_back_body,
        grid=(t // n,),
        in_specs=[smem(), smem(), pl.BlockSpec(memory_space=pl.ANY)],
        out_specs=[pl.BlockSpec(memory_space=pl.ANY), pl.BlockSpec(memory_space=pl.ANY)],
        out_shape=[out, out],
        scratch_shapes=[pltpu.SemaphoreType.DMA(())],
        compiler_params=_params(("arbitrary",)),
        name="moe_gather_back",
    )(slot1, slot2, y)


def _experts_body(te_ref, valid_ref, xs_ref, wg_ref, wu_ref, wd_ref, o_ref, xb):
    j = pl.program_id(0)
    f = pl.program_id(1)

    @pl.when(f == 0)
    def _():
        xb[...] = xs_ref[...].astype(BF16)
        o_ref[...] = jnp.zeros_like(o_ref)

    @pl.when(valid_ref[j] != 0)
    def _():
        hg = jnp.dot(xb[...], wg_ref[0], preferred_element_type=F32)
        hu = jnp.dot(xb[...], wu_ref[0], preferred_element_type=F32)
        o_ref[...] += jnp.dot((_silu(hg) * hu).astype(BF16), wd_ref[0], preferred_element_type=F32)


def _experts(xs, tile_expert, tile_valid, wg, wu, wd):
    rows = xs.shape[0]
    tm = MOE_TM
    ffs = wg.shape[-1] // MOE_FF_SPLIT
    grid_spec = pltpu.PrefetchScalarGridSpec(
        num_scalar_prefetch=2,
        grid=(rows // tm, MOE_FF_SPLIT),
        in_specs=[
            pl.BlockSpec((tm, D_MODEL), lambda j, f, te, va: (j, 0)),
            pl.BlockSpec((1, D_MODEL, ffs), lambda j, f, te, va: (te[j], 0, f)),
            pl.BlockSpec((1, D_MODEL, ffs), lambda j, f, te, va: (te[j], 0, f)),
            pl.BlockSpec((1, ffs, D_MODEL), lambda j, f, te, va: (te[j], f, 0)),
        ],
        out_specs=pl.BlockSpec((tm, D_MODEL), lambda j, f, te, va: (j, 0)),
        scratch_shapes=[pltpu.VMEM((tm, D_MODEL), BF16)],
    )
    return pl.pallas_call(
        _experts_body,
        grid_spec=grid_spec,
        out_shape=jax.ShapeDtypeStruct((rows, D_MODEL), F32),
        compiler_params=_params(("arbitrary", "arbitrary")),
        name="moe_experts",
    )(tile_expert, tile_valid, xs, wg, wu, wd)


def _combine_body(x_ref, y1_ref, y2_ref, wts_ref, g_ref, b_ref, o_ref):
    w = wts_ref[...]
    f = w[:, 0:1] * y1_ref[...] + w[:, 1:2] * y2_ref[...]
    o_ref[...] = _layer_norm(ALPHA * x_ref[...] + f, g_ref[...], b_ref[...])


def _combine(x, y1, y2, wts, g, b, tm=512):
    t = x.shape[0]
    tm = min(tm, t)
    rowb = lambda w: pl.BlockSpec((tm, w), lambda m: (m, 0))
    full = lambda a: pl.BlockSpec(a.shape, lambda m: (0, 0))
    return pl.pallas_call(
        _combine_body,
        grid=(t // tm,),
        in_specs=[rowb(D_MODEL), rowb(D_MODEL), rowb(D_MODEL), rowb(LANES), full(g), full(b)],
        out_specs=rowb(D_MODEL),
        out_shape=jax.ShapeDtypeStruct((t, D_MODEL), F32),
        compiler_params=_params(("arbitrary",)),
        name="moe_combine_ln",
    )(x, y1, y2, wts, g, b)


def _moe(x, wr_pad, wg, wu, wd, g, b):
    t = x.shape[0]
    wts, meta, cnt = _router(x, wr_pad)
    counts = cnt[0, :N_EXPERTS]
    padded = ((counts + MOE_TM - 1) // MOE_TM) * MOE_TM
    seg_end = jnp.cumsum(padded)
    seg_off = seg_end - padded
    experts = jnp.arange(N_EXPERTS, dtype=jnp.int32)

    def slot(idx, rank):
        return jnp.sum(jnp.where(idx[:, None] == experts[None, :], seg_off[None, :], 0), axis=1) + rank

    slot1 = slot(meta[:, 0], meta[:, 2]).astype(jnp.int32)
    slot2 = slot(meta[:, 1], meta[:, 3]).astype(jnp.int32)
    n_tiles = (2 * t) // MOE_TM + N_EXPERTS
    tile_start = jnp.arange(n_tiles, dtype=jnp.int32) * MOE_TM
    tile_expert = jnp.minimum(jnp.sum(tile_start[:, None] >= seg_end[None, :], axis=1), N_EXPERTS - 1)
    tile_valid = (tile_start < seg_end[-1]).astype(jnp.int32)

    xs = _dispatch(x, slot1, slot2, jnp.zeros((n_tiles * MOE_TM, D_MODEL), F32))
    ys = _experts(xs, tile_expert.astype(jnp.int32), tile_valid, wg, wu, wd)
    y1, y2 = _gather_back(ys, slot1, slot2, t)
    return _combine(x, y1, y2, wts, g, b)


def _pad_lanes(v):
    return jnp.pad(v, ((0, 0), (0, LANES - v.shape[-1])))


def _mixer(x, xb, bsz, seqlen, w_in, conv_dw_w, conv_dw_b, conv_ln_g, conv_ln_b, conv_w_out,
           ssm_conv_w, ssm_conv_b, ssm_dt_bias, ssm_a_log, ssm_d, ssm_norm_w, ssm_w_out, w_out, ln_g, ln_b):
    w_in_b = w_in.astype(BF16)
    proj = _matmul(xb, w_in_b, n_blocks=MAIN_W // 1024, tn=1024, out_dtype=BF16)
    dt_raw = _matmul(xb, _pad_lanes(w_in_b[:, COL_DT:COL_GATES]), n_blocks=1, tn=LANES, out_dtype=F32)
    gates = _matmul(xb, w_in_b[:, COL_GATES:], n_blocks=2, tn=1024, out_dtype=BF16)
    y_conv = _conv_branch(proj, conv_dw_w, conv_dw_b[None], conv_ln_g[None], conv_ln_b[None],
                          conv_w_out.astype(BF16), bsz, seqlen)
    y_n = _ssd_branch(proj, dt_raw, ssm_conv_w, ssm_conv_b[None], _pad_lanes(ssm_dt_bias[None]),
                      _pad_lanes(ssm_a_log[None]), jnp.repeat(ssm_d, HEADDIM)[None], ssm_norm_w[None],
                      bsz, seqlen)
    return _merge(x, y_conv, y_n, gates, ssm_w_out.astype(BF16), w_out.astype(BF16), ln_g[None], ln_b[None])


def kernel(x, mix_w_in, conv_dw_w, conv_dw_b, conv_ln_g, conv_ln_b, conv_w_out, ssm_conv_w, ssm_conv_b,
           ssm_dt_bias, ssm_a_log, ssm_d, ssm_norm_w, ssm_w_out, mix_w_out, ln_mix_g, ln_mix_b,
           ffn_w_gate, ffn_w_up, ffn_w_down, moe_router, moe_w_gate, moe_w_up, moe_w_down,
           ln_ffn_g, ln_ffn_b):
    bsz, seqlen, d = x.shape
    xf = x.reshape(bsz * seqlen, d)
    xb = xf.astype(BF16)
    for i in range(DEPTH):
        xf, xb = _mixer(xf, xb, bsz, seqlen, mix_w_in[i], conv_dw_w[i], conv_dw_b[i], conv_ln_g[i],
                        conv_ln_b[i], conv_w_out[i], ssm_conv_w[i], ssm_conv_b[i], ssm_dt_bias[i],
                        ssm_a_log[i], ssm_d[i], ssm_norm_w[i], ssm_w_out[i], mix_w_out[i],
                        ln_mix_g[i], ln_mix_b[i])
        j = i // 2
        if i % 2 == 0:
            xf, xb = _ffn(xf, xb, ffn_w_gate[j].astype(BF16), ffn_w_up[j].astype(BF16),
                          ffn_w_down[j].astype(BF16), ln_ffn_g[i][None], ln_ffn_b[i][None])
        else:
            xf = _moe(xf, _pad_lanes(moe_router[j]), moe_w_gate[j].astype(BF16), moe_w_up[j].astype(BF16),
                      moe_w_down[j].astype(BF16), ln_ffn_g[i][None], ln_ffn_b[i][None])
            xb = xf.astype(BF16)
    return xf.reshape(bsz, seqlen, d)
```

```python
import functools
import math

import jax
import jax.numpy as jnp
from jax import lax
from jax.experimental import pallas as pl
from jax.experimental.pallas import tpu as pltpu

F32 = jnp.float32
BF16 = jnp.bfloat16

D_MODEL = 1024
DEPTH = 2
CONV_K = 31
D_INNER = 2048
HEADDIM = 64
HEADS = D_INNER // HEADDIM
GROUPS = 4
HEADS_PER_GROUP = HEADS // GROUPS
STATE = 128
SSM_K = 4
CHUNK = 128
GROUP_W = D_INNER // GROUPS
BC_W = 2 * GROUPS * STATE
N_EXPERTS = 8
LN_EPS = 1e-5
RMS_EPS = 1e-5
ALPHA = (2 * DEPTH) ** 0.25

COL_Z = 2 * D_MODEL
COL_XS = COL_Z + D_INNER
COL_BC = COL_XS + D_INNER
COL_DT = COL_BC + BC_W
COL_GATES = COL_DT + HEADS
MAIN_W = COL_DT

LANES = 128
SUBLANES = 8
VMEM_LIMIT = 56 * 1024 * 1024


def _params(sem):
    return pltpu.CompilerParams(dimension_semantics=sem, vmem_limit_bytes=VMEM_LIMIT)


def _layer_norm(v, g, b):
    mu = jnp.mean(v, axis=-1, keepdims=True)
    d = v - mu
    var = jnp.mean(d * d, axis=-1, keepdims=True)
    return d * lax.rsqrt(var + LN_EPS) * g + b


def _silu(v):
    half = 0.5 * v
    return half + half * jnp.tanh(half)


def _mm_body(x_ref, w_ref, o_ref, wb):
    @pl.when(pl.program_id(1) == 0)
    def _():
        wb[...] = w_ref[...].astype(BF16)

    o_ref[...] = jnp.dot(x_ref[...], wb[...], preferred_element_type=F32).astype(o_ref.dtype)


def _matmul(x, w, *, n_blocks, tn, out_dtype, col_block0=0, tm=1024):
    t, k = x.shape
    tm = min(tm, t)
    return pl.pallas_call(
        _mm_body,
        grid=(n_blocks, t // tm),
        in_specs=[
            pl.BlockSpec((tm, k), lambda n, m: (m, 0)),
            pl.BlockSpec((k, tn), lambda n, m: (0, n + col_block0)),
        ],
        out_specs=pl.BlockSpec((tm, tn), lambda n, m: (m, n)),
        out_shape=jax.ShapeDtypeStruct((t, n_blocks * tn), out_dtype),
        scratch_shapes=[pltpu.VMEM((k, tn), BF16)],
        compiler_params=_params(("arbitrary", "arbitrary")),
        name="proj_matmul",
    )(x, w)


CONV_TL = 256
CONV_HALO = 32
CONV_RB = 64


def _conv_body(cv_ref, cg_ref, dww_ref, dwb_ref, lng_ref, lnb_ref, wout_ref, o_ref, ubuf, cbuf):
    l = pl.program_id(1)

    @pl.when(l == 0)
    def _():
        ubuf[0, 0:CONV_HALO, :] = jnp.zeros((CONV_HALO, D_MODEL), F32)

    @pl.when(l > 0)
    def _():
        ubuf[0, 0:CONV_HALO, :] = ubuf[0, CONV_TL:CONV_TL + CONV_HALO, :]

    cv = cv_ref[...].astype(F32)
    cg = cg_ref[...].astype(F32)
    ubuf[0, CONV_HALO:CONV_HALO + CONV_TL, :] = cv * jax.nn.sigmoid(cg)

    nshift = CONV_HALO + CONV_TL - SUBLANES
    for c in range(D_MODEL // LANES):
        cs = slice(c * LANES, (c + 1) * LANES)
        for j in range(1, SUBLANES):
            ubuf[j, 0:nshift, cs] = ubuf[0, j:j + nshift, cs]

    base = CONV_HALO - (CONV_K - 1)
    for c in range(D_MODEL // LANES):
        cs = slice(c * LANES, (c + 1) * LANES)
        wc = dww_ref[:, cs]
        bc = dwb_ref[:, cs]
        for r in range(CONV_TL // CONV_RB):
            acc = jnp.broadcast_to(bc, (CONV_RB, LANES))
            for k in range(CONV_K):
                j = (base + k) % SUBLANES
                r0 = base + k - j + r * CONV_RB
                acc = acc + wc[k:k + 1, :] * ubuf[j, r0:r0 + CONV_RB, cs]
            cbuf[r * CONV_RB:(r + 1) * CONV_RB, cs] = acc

    v = _layer_norm(cbuf[...], lng_ref[...], lnb_ref[...])
    s = _silu(v).astype(BF16)
    o_ref[...] = jnp.dot(s, wout_ref[...], preferred_element_type=F32).astype(o_ref.dtype)


def _conv_branch(proj, dww, dwb, lng, lnb, wout, bsz, seqlen):
    nl = seqlen // CONV_TL
    row = lambda b, l: b * nl + l
    const = lambda b, l: (0, 0)
    return pl.pallas_call(
        _conv_body,
        grid=(bsz, nl),
        in_specs=[
            pl.BlockSpec((CONV_TL, D_MODEL), lambda b, l: (row(b, l), 0)),
            pl.BlockSpec((CONV_TL, D_MODEL), lambda b, l: (row(b, l), 1)),
            pl.BlockSpec((CONV_K, D_MODEL), const),
            pl.BlockSpec((1, D_MODEL), const),
            pl.BlockSpec((1, D_MODEL), const),
            pl.BlockSpec((1, D_MODEL), const),
            pl.BlockSpec((D_MODEL, D_MODEL), const),
        ],
        out_specs=pl.BlockSpec((CONV_TL, D_MODEL), lambda b, l: (row(b, l), 0)),
        out_shape=jax.ShapeDtypeStruct((bsz * seqlen, D_MODEL), BF16),
        scratch_shapes=[
            pltpu.VMEM((SUBLANES, CONV_HALO + CONV_TL, D_MODEL), F32),
            pltpu.VMEM((CONV_TL, D_MODEL), F32),
        ],
        compiler_params=_params(("arbitrary", "arbitrary")),
        name="conv_branch",
    )(proj, proj, dww, dwb, lng, lnb, wout)


SSD_LANE_BLK = 512


SSD_HALO = 16


def _conv4_silu(cur_ref, ext, w_ref, b_ref, c, store):
    width = cur_ref.shape[1]

    @pl.when(c == 0)
    def _():
        ext[0:SSD_HALO, :] = jnp.zeros((SSD_HALO, width), BF16)

    @pl.when(c > 0)
    def _():
        ext[0:SSD_HALO, :] = ext[CHUNK:CHUNK + SSD_HALO, :]

    ext[SSD_HALO:SSD_HALO + CHUNK, :] = cur_ref[...]
    n_shift = SSM_K - 1
    r = lax.broadcasted_iota(jnp.int32, (n_shift * CHUNK, SSD_HALO + CHUNK), 0)
    col = lax.broadcasted_iota(jnp.int32, (n_shift * CHUNK, SSD_HALO + CHUNK), 1)
    sel = jnp.zeros(r.shape, F32)
    for k in range(n_shift):
        in_rows = jnp.logical_and(r >= k * CHUNK, r < (k + 1) * CHUNK)
        hit = jnp.logical_and(in_rows, col == r - k * CHUNK + SSD_HALO - n_shift + k)
        sel = jnp.where(hit, 1.0, sel)
    sel = sel.astype(BF16)
    for i in range(width // SSD_LANE_BLK):
        ls = slice(i * SSD_LANE_BLK, (i + 1) * SSD_LANE_BLK)
        shifted = jnp.dot(sel, ext[:, ls], preferred_element_type=F32)
        out = b_ref[:, ls] + w_ref[n_shift:n_shift + 1, ls] * ext[SSD_HALO:SSD_HALO + CHUNK, ls].astype(F32)
        for k in range(n_shift):
            out = out + w_ref[k:k + 1, ls] * shifted[k * CHUNK:(k + 1) * CHUNK, :]
        store(ls, _silu(out))


def _ssd_body(xs_ref, bc_ref, z_ref, dt_ref, cwx_ref, cbx_ref, cwbc_ref, cbbc_ref, dtb_ref, alog_ref,
              dvec_ref, nw_ref, o_ref, ext_x, ext_bc, xs_f, xs_b, bc_f, state, ybuf):
    c = pl.program_id(1)

    @pl.when(c == 0)
    def _():
        state[...] = jnp.zeros_like(state)

    def store_xs(ls, v):
        xs_f[:, ls] = v
        xs_b[:, ls] = v.astype(BF16)

    def store_bc(ls, v):
        bc_f[:, ls] = v

    _conv4_silu(xs_ref, ext_x, cwx_ref, cbx_ref, c, store_xs)
    _conv4_silu(bc_ref, ext_bc, cwbc_ref, cbbc_ref, c, store_bc)

    dt_raw = dt_ref[...] + dtb_ref[...]
    dt = jnp.maximum(dt_raw, 0.0) + jnp.log1p(jnp.exp(-jnp.abs(dt_raw)))
    da = dt * (-jnp.exp(alog_ref[...]))
    row = lax.broadcasted_iota(jnp.int32, (CHUNK, CHUNK), 0)
    lane = lax.broadcasted_iota(jnp.int32, (CHUNK, CHUNK), 1)
    causal = row >= lane
    first_head = lane < HEADDIM
    acum = jnp.dot(causal.astype(F32), da, preferred_element_type=F32,
                   precision=lax.Precision.HIGHEST)
    eacum = jnp.exp(acum)
    acum_t = acum.T
    dt_t = dt.T
    src_t = acum_t - jnp.log(dt_t)
    last = acum_t[:, CHUNK - 1:CHUNK]
    wt_t = jnp.exp(last - acum_t) * dt_t
    chunk_decay = eacum[CHUNK - 1:CHUNK, :]

    def block_diag(v):
        zero = jnp.zeros_like(v)
        return jnp.concatenate([jnp.where(first_head, v, zero), jnp.where(first_head, zero, v)], axis=0)

    for g in range(GROUPS):
        bg = bc_f[:, g * STATE:(g + 1) * STATE]
        cg = bc_f[:, GROUPS * STATE + g * STATE:GROUPS * STATE + (g + 1) * STATE]
        cb = lax.dot_general(cg.astype(BF16), bg.astype(BF16), (((1,), (1,)), ((), ())),
                             preferred_element_type=F32)
        bg_t = bg.T
        for q in range(HEADS_PER_GROUP // 2):
            pair = (g * HEADS_PER_GROUP) // 2 + q
            ps = slice(pair * 2 * HEADDIM, (pair + 1) * 2 * HEADDIM)
            x_bd = block_diag(xs_b[:, ps])
            prev = state[pair]
            s_bd = block_diag(prev.astype(BF16))
            m_parts, ce_parts, bw_parts = [], [], []
            for h in (2 * pair, 2 * pair + 1):
                colb = jnp.broadcast_to(acum[:, h:h + 1], (CHUNK, CHUNK))
                ecolb = jnp.broadcast_to(eacum[:, h:h + 1], (CHUNK, CHUNK))
                rowb = jnp.broadcast_to(src_t[h:h + 1, :], (CHUNK, CHUNK))
                m_parts.append((cb * jnp.exp(jnp.where(causal, colb - rowb, -jnp.inf))).astype(BF16))
                ce_parts.append((cg * ecolb).astype(BF16))
                bw_parts.append((bg_t * wt_t[h:h + 1, :]).astype(BF16))
            y2 = jnp.dot(jnp.concatenate(m_parts + ce_parts, axis=1), jnp.concatenate([x_bd, s_bd], axis=0),
                         preferred_element_type=F32)
            s2 = jnp.dot(jnp.concatenate(bw_parts, axis=1), x_bd, preferred_element_type=F32)
            h0 = 2 * pair
            cd2 = jnp.where(first_head[0:1, :], chunk_decay[:, h0:h0 + 1], chunk_decay[:, h0 + 1:h0 + 2])
            state[pair] = prev * cd2 + s2
            ybuf[:, ps] = y2

    for g in range(GROUPS):
        gs = slice(g * GROUP_W, (g + 1) * GROUP_W)
        y = ybuf[:, gs] + dvec_ref[:, gs] * xs_f[:, gs]
        hg = y * _silu(z_ref[:, gs].astype(F32))
        ms = jnp.mean(hg * hg, axis=-1, keepdims=True)
        o_ref[:, gs] = (hg * lax.rsqrt(ms + RMS_EPS) * nw_ref[:, gs]).astype(o_ref.dtype)


def _ssd_branch(proj, dt_raw, cw, cb, dtb, alog, dvec, nw, bsz, seqlen):
    nc = seqlen // CHUNK
    row = lambda b, c: b * nc + c
    const = lambda b, c: (0, 0)
    xs_blk = COL_XS // D_INNER
    bc_blk = COL_BC // BC_W
    z_blk = COL_Z // D_INNER
    return pl.pallas_call(
        _ssd_body,
        grid=(bsz, nc),
        in_specs=[
            pl.BlockSpec((CHUNK, D_INNER), lambda b, c: (row(b, c), xs_blk)),
            pl.BlockSpec((CHUNK, BC_W), lambda b, c: (row(b, c), bc_blk)),
            pl.BlockSpec((CHUNK, D_INNER), lambda b, c: (row(b, c), z_blk)),
            pl.BlockSpec((CHUNK, LANES), lambda b, c: (row(b, c), 0)),
            pl.BlockSpec((SSM_K, D_INNER), lambda b, c: (0, 0)),
            pl.BlockSpec((1, D_INNER), lambda b, c: (0, 0)),
            pl.BlockSpec((SSM_K, BC_W), lambda b, c: (0, D_INNER // BC_W)),
            pl.BlockSpec((1, BC_W), lambda b, c: (0, D_INNER // BC_W)),
            pl.BlockSpec((1, LANES), const),
            pl.BlockSpec((1, LANES), const),
            pl.BlockSpec((1, D_INNER), const),
            pl.BlockSpec((1, D_INNER), const),
        ],
        out_specs=pl.BlockSpec((CHUNK, D_INNER), lambda b, c: (row(b, c), 0)),
        out_shape=jax.ShapeDtypeStruct((bsz * seqlen, D_INNER), BF16),
        scratch_shapes=[
            pltpu.VMEM((SSD_HALO + CHUNK, D_INNER), BF16),
            pltpu.VMEM((SSD_HALO + CHUNK, BC_W), BF16),
            pltpu.VMEM((CHUNK, D_INNER), F32),
            pltpu.VMEM((CHUNK, D_INNER), BF16),
            pltpu.VMEM((CHUNK, BC_W), F32),
            pltpu.VMEM((HEADS // 2, STATE, 2 * HEADDIM), F32),
            pltpu.VMEM((CHUNK, D_INNER), F32),
        ],
        compiler_params=_params(("arbitrary", "arbitrary")),
        name="ssd_branch",
    )(proj, proj, proj, dt_raw, cw, cb, cw, cb, dtb, alog, dvec, nw)


def _merge_body(x_ref, yc_ref, yn_ref, gt_ref, wssm_ref, wout_ref, g_ref, b_ref, o_ref, ob_ref):
    y_ssm = jnp.dot(yn_ref[...], wssm_ref[...], preferred_element_type=F32)
    gates = gt_ref[...].astype(F32)
    hmix = (jax.nn.sigmoid(gates[:, :D_MODEL]) * yc_ref[...].astype(F32)
            + jax.nn.sigmoid(gates[:, D_MODEL:]) * y_ssm)
    mix = jnp.dot(hmix.astype(BF16), wout_ref[...], preferred_element_type=F32)
    out = _layer_norm(ALPHA * x_ref[...] + mix, g_ref[...], b_ref[...])
    o_ref[...] = out
    ob_ref[...] = out.astype(BF16)


def _merge(x, y_conv, y_n, gates, wssm, wout, g, b, tm=512):
    t = x.shape[0]
    tm = min(tm, t)
    rowb = lambda w: pl.BlockSpec((tm, w), lambda m: (m, 0))
    full = lambda a: pl.BlockSpec(a.shape, lambda m: (0, 0))
    return pl.pallas_call(
        _merge_body,
        grid=(t // tm,),
        in_specs=[rowb(D_MODEL), rowb(D_MODEL), rowb(D_INNER), rowb(2 * D_MODEL),
                  full(wssm), full(wout), full(g), full(b)],
        out_specs=[rowb(D_MODEL), rowb(D_MODEL)],
        out_shape=[jax.ShapeDtypeStruct((t, D_MODEL), F32), jax.ShapeDtypeStruct((t, D_MODEL), BF16)],
        compiler_params=_params(("arbitrary",)),
        name="merge_ln",
    )(x, y_conv, y_n, gates, wssm, wout, g, b)


def _ffn_body(x_ref, xb_ref, wg_ref, wu_ref, wd_ref, g_ref, b_ref, o_ref, ob_ref):
    xb = xb_ref[...]
    hg = jnp.dot(xb, wg_ref[...], preferred_element_type=F32)
    hu = jnp.dot(xb, wu_ref[...], preferred_element_type=F32)
    f = jnp.dot((_silu(hg) * hu).astype(BF16), wd_ref[...], preferred_element_type=F32)
    out = _layer_norm(ALPHA * x_ref[...] + f, g_ref[...], b_ref[...])
    o_ref[...] = out
    ob_ref[...] = out.astype(BF16)


def _ffn(x, xb, wg, wu, wd, g, b, tm=512):
    t = x.shape[0]
    tm = min(tm, t)
    rowb = lambda w: pl.BlockSpec((tm, w), lambda m: (m, 0))
    full = lambda a: pl.BlockSpec(a.shape, lambda m: (0, 0), pipeline_mode=pl.Buffered(1))
    return pl.pallas_call(
        _ffn_body,
        grid=(t // tm,),
        in_specs=[rowb(D_MODEL), rowb(D_MODEL), full(wg), full(wu), full(wd), full(g), full(b)],
        out_specs=[rowb(D_MODEL), rowb(D_MODEL)],
        out_shape=[jax.ShapeDtypeStruct((t, D_MODEL), F32), jax.ShapeDtypeStruct((t, D_MODEL), BF16)],
        compiler_params=_params(("arbitrary",)),
        name="ffn_ln",
    )(x, xb, wg, wu, wd, g, b)


def _router_body(x_ref, wr_ref, wts_ref, meta_ref, cnt_ref, carry):
    @pl.when(pl.program_id(0) == 0)
    def _():
        carry[...] = jnp.zeros_like(carry)

    logits = jnp.dot(x_ref[...], wr_ref[...], preferred_element_type=F32, precision=lax.Precision.HIGHEST)
    tm = logits.shape[0]
    lane = lax.broadcasted_iota(jnp.int32, logits.shape, 1)
    logits = jnp.where(lane < N_EXPERTS, logits, -jnp.inf)
    m1 = jnp.max(logits, axis=-1, keepdims=True)
    i1 = jnp.min(jnp.where(logits == m1, lane, LANES), axis=-1, keepdims=True)
    rest = jnp.where(lane == i1, -jnp.inf, logits)
    m2 = jnp.max(rest, axis=-1, keepdims=True)
    i2 = jnp.min(jnp.where(rest == m2, lane, LANES), axis=-1, keepdims=True)
    e2 = jnp.exp(m2 - m1)
    denom = 1.0 + e2
    wts_ref[...] = jnp.where(lane == 0, 1.0 / denom, jnp.where(lane == 1, e2 / denom, 0.0))

    chosen = jnp.logical_or(lane == i1, lane == i2)
    chosen_f = jnp.where(chosen, 1.0, 0.0)
    r = lax.broadcasted_iota(jnp.int32, (tm, tm), 0)
    c = lax.broadcasted_iota(jnp.int32, (tm, tm), 1)
    earlier = jnp.where(c < r, 1.0, 0.0).astype(BF16)
    prefix = jnp.dot(earlier, chosen_f.astype(BF16), preferred_element_type=F32) + carry[...]
    rank1 = jnp.sum(jnp.where(lane == i1, prefix, 0.0), axis=-1, keepdims=True).astype(jnp.int32)
    rank2 = jnp.sum(jnp.where(lane == i2, prefix, 0.0), axis=-1, keepdims=True).astype(jnp.int32)
    meta_ref[...] = jnp.where(lane == 0, i1, jnp.where(lane == 1, i2,
                              jnp.where(lane == 2, rank1, jnp.where(lane == 3, rank2, 0))))
    total = carry[...] + jnp.sum(chosen_f, axis=0, keepdims=True)
    carry[...] = total
    cnt_ref[...] = jnp.broadcast_to(total, cnt_ref.shape).astype(jnp.int32)


def _router(x, wr_pad, tm=1024):
    t = x.shape[0]
    tm = min(tm, t)
    return pl.pallas_call(
        _router_body,
        grid=(t // tm,),
        in_specs=[pl.BlockSpec((tm, D_MODEL), lambda m: (m, 0)),
                  pl.BlockSpec((D_MODEL, LANES), lambda m: (0, 0))],
        out_specs=[pl.BlockSpec((tm, LANES), lambda m: (m, 0)),
                   pl.BlockSpec((tm, LANES), lambda m: (m, 0)),
                   pl.BlockSpec((SUBLANES, LANES), lambda m: (0, 0))],
        out_shape=[jax.ShapeDtypeStruct((t, LANES), F32),
                   jax.ShapeDtypeStruct((t, LANES), jnp.int32),
                   jax.ShapeDtypeStruct((SUBLANES, LANES), jnp.int32)],
        scratch_shapes=[pltpu.VMEM((1, LANES), F32)],
        compiler_params=_params(("arbitrary",)),
        name="router",
    )(x, wr_pad)


MOE_FF_SPLIT = 2


MOE_TM = 512
MOE_ROWS = 512
DMA_UNROLL = 8


def _row_copy(src, dst, src_row, dst_row, sem):
    return pltpu.make_async_copy(src.at[pl.ds(src_row, 1)], dst.at[pl.ds(dst_row, 1)], sem)


def _dispatch_body(slot1_ref, slot2_ref, x_ref, xs_in_hbm, xs_hbm, sem):
    del xs_in_hbm
    n = slot1_ref.shape[0]

    def issue(t, carry):
        _row_copy(x_ref, xs_hbm, t, slot1_ref[t], sem).start()
        _row_copy(x_ref, xs_hbm, t, slot2_ref[t], sem).start()
        return carry

    lax.fori_loop(0, n, issue, 0, unroll=DMA_UNROLL)

    def drain(t, carry):
        _row_copy(x_ref, xs_hbm, 0, 0, sem).wait()
        _row_copy(x_ref, xs_hbm, 0, 0, sem).wait()
        return carry

    lax.fori_loop(0, n, drain, 0, unroll=DMA_UNROLL)


def _dispatch(x, slot1, slot2, xs_zero):
    t = x.shape[0]
    n = min(MOE_ROWS, t)
    smem = lambda: pl.BlockSpec((n,), lambda i: (i,), memory_space=pltpu.SMEM)
    return pl.pallas_call(
        _dispatch_body,
        grid=(t // n,),
        in_specs=[smem(), smem(), pl.BlockSpec((n, D_MODEL), lambda i: (i, 0)),
                  pl.BlockSpec(memory_space=pl.ANY)],
        out_specs=pl.BlockSpec(memory_space=pl.ANY),
        out_shape=jax.ShapeDtypeStruct(xs_zero.shape, xs_zero.dtype),
        scratch_shapes=[pltpu.SemaphoreType.DMA(())],
        input_output_aliases={3: 0},
        compiler_params=_params(("arbitrary",)),
        name="moe_dispatch",
    )(slot1, slot2, x, xs_zero)


def _experts_body(te_ref, valid_ref, xs_ref, wg_ref, wu_ref, wd_ref, o_ref, xb):
    j = pl.program_id(0)
    f = pl.program_id(1)

    @pl.when(f == 0)
    def _():
        xb[...] = xs_ref[...].astype(BF16)
        o_ref[...] = jnp.zeros_like(o_ref)

    @pl.when(valid_ref[j] != 0)
    def _():
        hg = jnp.dot(xb[...], wg_ref[0], preferred_element_type=F32)
        hu = jnp.dot(xb[...], wu_ref[0], preferred_element_type=F32)
        o_ref[...] += jnp.dot((_silu(hg) * hu).astype(BF16), wd_ref[0], preferred_element_type=F32)


def _experts(xs, tile_expert, tile_valid, wg, wu, wd):
    rows = xs.shape[0]
    tm = MOE_TM
    ffs = wg.shape[-1] // MOE_FF_SPLIT
    grid_spec = pltpu.PrefetchScalarGridSpec(
        num_scalar_prefetch=2,
        grid=(rows // tm, MOE_FF_SPLIT),
        in_specs=[
            pl.BlockSpec((tm, D_MODEL), lambda j, f, te, va: (j, 0)),
            pl.BlockSpec((1, D_MODEL, ffs), lambda j, f, te, va: (te[j], 0, f)),
            pl.BlockSpec((1, D_MODEL, ffs), lambda j, f, te, va: (te[j], 0, f)),
            pl.BlockSpec((1, ffs, D_MODEL), lambda j, f, te, va: (te[j], f, 0)),
        ],
        out_specs=pl.BlockSpec((tm, D_MODEL), lambda j, f, te, va: (j, 0)),
        scratch_shapes=[pltpu.VMEM((tm, D_MODEL), BF16)],
    )
    return pl.pallas_call(
        _experts_body,
        grid_spec=grid_spec,
        out_shape=jax.ShapeDtypeStruct((rows, D_MODEL), F32),
        compiler_params=_params(("arbitrary", "arbitrary")),
        name="moe_experts",
    )(tile_expert, tile_valid, xs, wg, wu, wd)


def _combine_body(slot1_ref, slot2_ref, x_ref, wts_ref, g_ref, b_ref, y_hbm, o_ref, y1, y2, sem):
    n = slot1_ref.shape[0]

    def issue(t, carry):
        _row_copy(y_hbm, y1, slot1_ref[t], t, sem).start()
        _row_copy(y_hbm, y2, slot2_ref[t], t, sem).start()
        return carry

    lax.fori_loop(0, n, issue, 0, unroll=DMA_UNROLL)

    def drain(t, carry):
        _row_copy(y_hbm, y1, 0, 0, sem).wait()
        _row_copy(y_hbm, y2, 0, 0, sem).wait()
        return carry

    lax.fori_loop(0, n, drain, 0, unroll=DMA_UNROLL)

    w = wts_ref[...]
    f = w[:, 0:1] * y1[...] + w[:, 1:2] * y2[...]
    o_ref[...] = _layer_norm(ALPHA * x_ref[...] + f, g_ref[...], b_ref[...])


def _combine(x, ys, slot1, slot2, wts, g, b):
    t = x.shape[0]
    n = min(MOE_ROWS, t)
    smem = lambda: pl.BlockSpec((n,), lambda i: (i,), memory_space=pltpu.SMEM)
    rowb = lambda w: pl.BlockSpec((n, w), lambda i: (i, 0))
    full = lambda a: pl.BlockSpec(a.shape, lambda i: (0, 0))
    return pl.pallas_call(
        _combine_body,
        grid=(t // n,),
        in_specs=[smem(), smem(), rowb(D_MODEL), rowb(LANES), full(g), full(b),
                  pl.BlockSpec(memory_space=pl.ANY)],
        out_specs=rowb(D_MODEL),
        out_shape=jax.ShapeDtypeStruct((t, D_MODEL), F32),
        scratch_shapes=[pltpu.VMEM((n, D_MODEL), F32), pltpu.VMEM((n, D_MODEL), F32),
                        pltpu.SemaphoreType.DMA(())],
        compiler_params=_params(("arbitrary",)),
        name="moe_combine_ln",
    )(slot1, slot2, x, wts, g, b, ys)


def _moe(x, wr_pad, wg, wu, wd, g, b):
    t = x.shape[0]
    wts, meta, cnt = _router(x, wr_pad)
    counts = cnt[0, :N_EXPERTS]
    padded = ((counts + MOE_TM - 1) // MOE_TM) * MOE_TM
    seg_end = jnp.cumsum(padded)
    seg_off = seg_end - padded
    experts = jnp.arange(N_EXPERTS, dtype=jnp.int32)

    def slot(idx, rank):
        return jnp.sum(jnp.where(idx[:, None] == experts[None, :], seg_off[None, :], 0), axis=1) + rank

    slot1 = slot(meta[:, 0], meta[:, 2]).astype(jnp.int32)
    slot2 = slot(meta[:, 1], meta[:, 3]).astype(jnp.int32)
    n_tiles = (2 * t) // MOE_TM + N_EXPERTS
    tile_start = jnp.arange(n_tiles, dtype=jnp.int32) * MOE_TM
    tile_expert = jnp.minimum(jnp.sum(tile_start[:, None] >= seg_end[None, :], axis=1), N_EXPERTS - 1)
    tile_valid = (tile_start < seg_end[-1]).astype(jnp.int32)

    xs = _dispatch(x, slot1, slot2, jnp.zeros((n_tiles * MOE_TM, D_MODEL), F32))
    ys = _experts(xs, tile_expert.astype(jnp.int32), tile_valid, wg, wu, wd)
    return _combine(x, ys, slot1, slot2, wts, g, b)


def _pad_lanes(v):
    return jnp.pad(v, ((0, 0), (0, LANES - v.shape[-1])))


def _mixer(x, xb, bsz, seqlen, w_in, conv_dw_w, conv_dw_b, conv_ln_g, conv_ln_b, conv_w_out,
           ssm_conv_w, ssm_conv_b, ssm_dt_bias, ssm_a_log, ssm_d, ssm_norm_w, ssm_w_out, w_out, ln_g, ln_b):
    proj = _matmul(xb, w_in, n_blocks=MAIN_W // 1024, tn=1024, out_dtype=BF16)
    dt_raw = _matmul(xb, _pad_lanes(w_in[:, COL_DT:COL_GATES]), n_blocks=1, tn=LANES, out_dtype=F32)
    gates = _matmul(xb, w_in[:, COL_GATES:], n_blocks=2, tn=1024, out_dtype=BF16)
    y_conv = _conv_branch(proj, conv_dw_w, conv_dw_b[None], conv_ln_g[None], conv_ln_b[None],
                          conv_w_out.astype(BF16), bsz, seqlen)
    y_n = _ssd_branch(proj, dt_raw, ssm_conv_w, ssm_conv_b[None], _pad_lanes(ssm_dt_bias[None]),
                      _pad_lanes(ssm_a_log[None]), jnp.repeat(ssm_d, HEADDIM)[None], ssm_norm_w[None],
                      bsz, seqlen)
    return _merge(x, y_conv, y_n, gates, ssm_w_out.astype(BF16), w_out.astype(BF16), ln_g[None], ln_b[None])


def kernel(x, mix_w_in, conv_dw_w, conv_dw_b, conv_ln_g, conv_ln_b, conv_w_out, ssm_conv_w, ssm_conv_b,
           ssm_dt_bias, ssm_a_log, ssm_d, ssm_norm_w, ssm_w_out, mix_w_out, ln_mix_g, ln_mix_b,
           ffn_w_gate, ffn_w_up, ffn_w_down, moe_router, moe_w_gate, moe_w_up, moe_w_down,
           ln_ffn_g, ln_ffn_b):
    bsz, seqlen, d = x.shape
    xf = x.reshape(bsz * seqlen, d)
    xb = xf.astype(BF16)
    for i in range(DEPTH):
        xf, xb = _mixer(xf, xb, bsz, seqlen, mix_w_in[i], conv_dw_w[i], conv_dw_b[i], conv_ln_g[i],
                        conv_ln_b[i], conv_w_out[i], ssm_conv_w[i], ssm_conv_b[i], ssm_dt_bias[i],
                        ssm_a_log[i], ssm_d[i], ssm_norm_w[i], ssm_w_out[i], mix_w_out[i],
                        ln_mix_g[i], ln_mix_b[i])
        j = i // 2
        if i % 2 == 0:
            xf, xb = _ffn(xf, xb, ffn_w_gate[j].astype(BF16), ffn_w_up[j].astype(BF16),
                          ffn_w_down[j].astype(BF16), ln_ffn_g[i][None], ln_ffn_b[i][None])
        else:
            xf = _moe(xf, _pad_lanes(moe_router[j]), moe_w_gate[j].astype(BF16), moe_w_up[j].astype(BF16),
                      moe_w_down[j].astype(BF16), ln_ffn_g[i][None], ln_ffn_b[i][None])
            xb = xf.astype(BF16)
    return xf.reshape(bsz, seqlen, d)
```

```python
import functools
import math

import jax
import jax.numpy as jnp
from jax import lax
from jax.experimental import pallas as pl
from jax.experimental.pallas import tpu as pltpu

F32 = jnp.float32
BF16 = jnp.bfloat16

D_MODEL = 1024
DEPTH = 2
CONV_K = 31
D_INNER = 2048
HEADDIM = 64
HEADS = D_INNER // HEADDIM
GROUPS = 4
HEADS_PER_GROUP = HEADS // GROUPS
STATE = 128
SSM_K = 4
CHUNK = 128
GROUP_W = D_INNER // GROUPS
BC_W = 2 * GROUPS * STATE
N_EXPERTS = 8
LN_EPS = 1e-5
RMS_EPS = 1e-5
ALPHA = (2 * DEPTH) ** 0.25

COL_Z = 2 * D_MODEL
COL_XS = COL_Z + D_INNER
COL_BC = COL_XS + D_INNER
COL_DT = COL_BC + BC_W
COL_GATES = COL_DT + HEADS
MAIN_W = COL_DT

LANES = 128
SUBLANES = 8
VMEM_LIMIT = 56 * 1024 * 1024


def _params(sem):
    return pltpu.CompilerParams(dimension_semantics=sem, vmem_limit_bytes=VMEM_LIMIT)


def _layer_norm(v, g, b):
    mu = jnp.mean(v, axis=-1, keepdims=True)
    d = v - mu
    var = jnp.mean(d * d, axis=-1, keepdims=True)
    return d * lax.rsqrt(var + LN_EPS) * g + b


def _silu(v):
    half = 0.5 * v
    return half + half * jnp.tanh(half)


def _mm_body(x_ref, w_ref, o_ref, wb):
    @pl.when(pl.program_id(1) == 0)
    def _():
        wb[...] = w_ref[...].astype(BF16)

    o_ref[...] = jnp.dot(x_ref[...].astype(BF16), wb[...], preferred_element_type=F32).astype(o_ref.dtype)


def _matmul(x, w, *, n_blocks, tn, out_dtype, layer=None, tm=1024):
    t, k = x.shape
    tm = min(tm, t)
    if layer is None:
        w_spec = pl.BlockSpec((k, tn), lambda n, m: (0, n))
    else:
        w_spec = pl.BlockSpec((None, k, tn), lambda n, m: (layer, 0, n))
    return pl.pallas_call(
        _mm_body,
        grid=(n_blocks, t // tm),
        in_specs=[
            pl.BlockSpec((tm, k), lambda n, m: (m, 0)),
            w_spec,
        ],
        out_specs=pl.BlockSpec((tm, tn), lambda n, m: (m, n)),
        out_shape=jax.ShapeDtypeStruct((t, n_blocks * tn), out_dtype),
        scratch_shapes=[pltpu.VMEM((k, tn), BF16)],
        compiler_params=_params(("arbitrary", "arbitrary")),
        name="proj_matmul",
    )(x, w)


SSD_LANE_BLK = 512
SSD_HALO = 16


def _conv4_silu(cur_ref, ext, w_ref, b_ref, c, store):
    width = cur_ref.shape[1]

    @pl.when(c == 0)
    def _():
        ext[0:SSD_HALO, :] = jnp.zeros((SSD_HALO, width), BF16)

    @pl.when(c > 0)
    def _():
        ext[0:SSD_HALO, :] = ext[CHUNK:CHUNK + SSD_HALO, :]

    ext[SSD_HALO:SSD_HALO + CHUNK, :] = cur_ref[...]
    n_shift = SSM_K - 1
    r = lax.broadcasted_iota(jnp.int32, (n_shift * CHUNK, SSD_HALO + CHUNK), 0)
    col = lax.broadcasted_iota(jnp.int32, (n_shift * CHUNK, SSD_HALO + CHUNK), 1)
    sel = jnp.zeros(r.shape, F32)
    for k in range(n_shift):
        in_rows = jnp.logical_and(r >= k * CHUNK, r < (k + 1) * CHUNK)
        hit = jnp.logical_and(in_rows, col == r - k * CHUNK + SSD_HALO - n_shift + k)
        sel = jnp.where(hit, 1.0, sel)
    sel = sel.astype(BF16)
    for i in range(width // SSD_LANE_BLK):
        ls = slice(i * SSD_LANE_BLK, (i + 1) * SSD_LANE_BLK)
        shifted = jnp.dot(sel, ext[:, ls], preferred_element_type=F32)
        out = b_ref[:, ls] + w_ref[n_shift:n_shift + 1, ls] * ext[SSD_HALO:SSD_HALO + CHUNK, ls].astype(F32)
        for k in range(n_shift):
            out = out + w_ref[k:k + 1, ls] * shifted[k * CHUNK:(k + 1) * CHUNK, :]
        store(ls, _silu(out))


def _ssd_body(xs_ref, bc_ref, z_ref, dt_ref, cwx_ref, cbx_ref, cwbc_ref, cbbc_ref, dtb_ref, alog_ref,
              dvec_ref, nw_ref, o_ref, ext_x, ext_bc, xs_f, xs_b, bc_f, state, ybuf):
    c = pl.program_id(1)

    @pl.when(c == 0)
    def _():
        state[...] = jnp.zeros_like(state)

    def store_xs(ls, v):
        xs_f[:, ls] = v
        xs_b[:, ls] = v.astype(BF16)

    def store_bc(ls, v):
        bc_f[:, ls] = v

    _conv4_silu(xs_ref, ext_x, cwx_ref, cbx_ref, c, store_xs)
    _conv4_silu(bc_ref, ext_bc, cwbc_ref, cbbc_ref, c, store_bc)

    dt_raw = dt_ref[...] + dtb_ref[...]
    dt = jnp.maximum(dt_raw, 0.0) + jnp.log1p(jnp.exp(-jnp.abs(dt_raw)))
    da = dt * (-jnp.exp(alog_ref[...]))
    row = lax.broadcasted_iota(jnp.int32, (CHUNK, CHUNK), 0)
    lane = lax.broadcasted_iota(jnp.int32, (CHUNK, CHUNK), 1)
    causal = row >= lane
    first_head = lane < HEADDIM
    acum = jnp.dot(causal.astype(F32), da, preferred_element_type=F32,
                   precision=lax.Precision.HIGHEST)
    eacum = jnp.exp(acum)
    acum_t = acum.T
    dt_t = dt.T
    src_t = acum_t - jnp.log(dt_t)
    last = acum_t[:, CHUNK - 1:CHUNK]
    wt_t = jnp.exp(last - acum_t) * dt_t
    chunk_decay = eacum[CHUNK - 1:CHUNK, :]

    def block_diag(v):
        zero = jnp.zeros_like(v)
        return jnp.concatenate([jnp.where(first_head, v, zero), jnp.where(first_head, zero, v)], axis=0)

    for g in range(GROUPS):
        bg = bc_f[:, g * STATE:(g + 1) * STATE]
        cg = bc_f[:, GROUPS * STATE + g * STATE:GROUPS * STATE + (g + 1) * STATE]
        cb = lax.dot_general(cg.astype(BF16), bg.astype(BF16), (((1,), (1,)), ((), ())),
                             preferred_element_type=F32)
        bg_t = bg.T
        for q in range(HEADS_PER_GROUP // 2):
            pair = (g * HEADS_PER_GROUP) // 2 + q
            ps = slice(pair * 2 * HEADDIM, (pair + 1) * 2 * HEADDIM)
            x_bd = block_diag(xs_b[:, ps])
            prev = state[pair]
            s_bd = block_diag(prev.astype(BF16))
            m_parts, ce_parts, bw_parts = [], [], []
            for h in (2 * pair, 2 * pair + 1):
                colb = jnp.broadcast_to(acum[:, h:h + 1], (CHUNK, CHUNK))
                ecolb = jnp.broadcast_to(eacum[:, h:h + 1], (CHUNK, CHUNK))
                rowb = jnp.broadcast_to(src_t[h:h + 1, :], (CHUNK, CHUNK))
                m_parts.append((cb * jnp.exp(jnp.where(causal, colb - rowb, -jnp.inf))).astype(BF16))
                ce_parts.append((cg * ecolb).astype(BF16))
                bw_parts.append((bg_t * wt_t[h:h + 1, :]).astype(BF16))
            y2 = jnp.dot(jnp.concatenate(m_parts + ce_parts, axis=1), jnp.concatenate([x_bd, s_bd], axis=0),
                         preferred_element_type=F32)
            s2 = jnp.dot(jnp.concatenate(bw_parts, axis=1), x_bd, preferred_element_type=F32)
            h0 = 2 * pair
            cd2 = jnp.where(first_head[0:1, :], chunk_decay[:, h0:h0 + 1], chunk_decay[:, h0 + 1:h0 + 2])
            state[pair] = prev * cd2 + s2
            ybuf[:, ps] = y2

    for g in range(GROUPS):
        gs = slice(g * GROUP_W, (g + 1) * GROUP_W)
        y = ybuf[:, gs] + dvec_ref[:, gs] * xs_f[:, gs]
        hg = y * _silu(z_ref[:, gs].astype(F32))
        ms = jnp.mean(hg * hg, axis=-1, keepdims=True)
        o_ref[:, gs] = (hg * lax.rsqrt(ms + RMS_EPS) * nw_ref[:, gs]).astype(o_ref.dtype)


def _ssd_branch(proj, dt_raw, cw, cb, dtb, alog, dvec, nw, bsz, seqlen):
    nc = seqlen // CHUNK
    row = lambda b, c: b * nc + c
    const = lambda b, c: (0, 0)
    xs_blk = COL_XS // D_INNER
    bc_blk = COL_BC // BC_W
    z_blk = COL_Z // D_INNER
    return pl.pallas_call(
        _ssd_body,
        grid=(bsz, nc),
        in_specs=[
            pl.BlockSpec((CHUNK, D_INNER), lambda b, c: (row(b, c), xs_blk)),
            pl.BlockSpec((CHUNK, BC_W), lambda b, c: (row(b, c), bc_blk)),
            pl.BlockSpec((CHUNK, D_INNER), lambda b, c: (row(b, c), z_blk)),
            pl.BlockSpec((CHUNK, LANES), lambda b, c: (row(b, c), 0)),
            pl.BlockSpec((SSM_K, D_INNER), lambda b, c: (0, 0)),
            pl.BlockSpec((1, D_INNER), lambda b, c: (0, 0)),
            pl.BlockSpec((SSM_K, BC_W), lambda b, c: (0, D_INNER // BC_W)),
            pl.BlockSpec((1, BC_W), lambda b, c: (0, D_INNER // BC_W)),
            pl.BlockSpec((1, LANES), const),
            pl.BlockSpec((1, LANES), const),
            pl.BlockSpec((1, D_INNER), const),
            pl.BlockSpec((1, D_INNER), const),
        ],
        out_specs=pl.BlockSpec((CHUNK, D_INNER), lambda b, c: (row(b, c), 0)),
        out_shape=jax.ShapeDtypeStruct((bsz * seqlen, D_INNER), BF16),
        scratch_shapes=[
            pltpu.VMEM((SSD_HALO + CHUNK, D_INNER), BF16),
            pltpu.VMEM((SSD_HALO + CHUNK, BC_W), BF16),
            pltpu.VMEM((CHUNK, D_INNER), F32),
            pltpu.VMEM((CHUNK, D_INNER), BF16),
            pltpu.VMEM((CHUNK, BC_W), F32),
            pltpu.VMEM((HEADS // 2, STATE, 2 * HEADDIM), F32),
            pltpu.VMEM((CHUNK, D_INNER), F32),
        ],
        compiler_params=_params(("arbitrary", "arbitrary")),
        name="ssd_branch",
    )(proj, proj, proj, dt_raw, cw, cb, cw, cb, dtb, alog, dvec, nw)


CONV_TL = 256
CONV_HALO = 32
CONV_RB = 64


def _mix_body(x_ref, cv_ref, cg_ref, yn_ref, gt_ref, dww_ref, dwb_ref, lng_ref, lnb_ref, wconv_ref,
              wssm_ref, wout_ref, g_ref, b_ref, o_ref, ob_ref, ubuf, cbuf):
    l = pl.program_id(1)

    @pl.when(l == 0)
    def _():
        ubuf[0, 0:CONV_HALO, :] = jnp.zeros((CONV_HALO, D_MODEL), F32)

    @pl.when(l > 0)
    def _():
        ubuf[0, 0:CONV_HALO, :] = ubuf[0, CONV_TL:CONV_TL + CONV_HALO, :]

    y_ssm = jnp.dot(yn_ref[...], wssm_ref[...], preferred_element_type=F32)

    cv = cv_ref[...].astype(F32)
    cg = cg_ref[...].astype(F32)
    ubuf[0, CONV_HALO:CONV_HALO + CONV_TL, :] = cv * jax.nn.sigmoid(cg)

    nshift = CONV_HALO + CONV_TL - SUBLANES
    for c in range(D_MODEL // LANES):
        cs = slice(c * LANES, (c + 1) * LANES)
        for j in range(1, SUBLANES):
            ubuf[j, 0:nshift, cs] = ubuf[0, j:j + nshift, cs]

    base = CONV_HALO - (CONV_K - 1)
    for c in range(D_MODEL // LANES):
        cs = slice(c * LANES, (c + 1) * LANES)
        wc = dww_ref[:, cs]
        bc = dwb_ref[:, cs]
        for r in range(CONV_TL // CONV_RB):
            acc = jnp.broadcast_to(bc, (CONV_RB, LANES))
            for k in range(CONV_K):
                j = (base + k) % SUBLANES
                r0 = base + k - j + r * CONV_RB
                acc = acc + wc[k:k + 1, :] * ubuf[j, r0:r0 + CONV_RB, cs]
            cbuf[r * CONV_RB:(r + 1) * CONV_RB, cs] = acc

    v = _layer_norm(cbuf[...], lng_ref[...], lnb_ref[...])
    y_conv = jnp.dot(_silu(v).astype(BF16), wconv_ref[...], preferred_element_type=F32)

    gates = gt_ref[...].astype(F32)
    hmix = jax.nn.sigmoid(gates[:, :D_MODEL]) * y_conv + jax.nn.sigmoid(gates[:, D_MODEL:]) * y_ssm
    mix = jnp.dot(hmix.astype(BF16), wout_ref[...], preferred_element_type=F32)
    out = _layer_norm(ALPHA * x_ref[...] + mix, g_ref[...], b_ref[...])
    o_ref[...] = out
    ob_ref[...] = out.astype(BF16)


def _mix(x, proj, y_n, gates, dww, dwb, lng, lnb, wconv, wssm, wout, g, b, bsz, seqlen):
    nl = seqlen // CONV_TL
    t = bsz * seqlen
    rowb = lambda w, col=0: pl.BlockSpec((CONV_TL, w), lambda bb, l: (bb * nl + l, col))
    full = lambda a: pl.BlockSpec(a.shape, lambda bb, l: (0, 0), pipeline_mode=pl.Buffered(1))
    return pl.pallas_call(
        _mix_body,
        grid=(bsz, nl),
        in_specs=[rowb(D_MODEL), rowb(D_MODEL, 0), rowb(D_MODEL, 1), rowb(D_INNER), rowb(2 * D_MODEL),
                  full(dww), full(dwb), full(lng), full(lnb), full(wconv), full(wssm), full(wout),
                  full(g), full(b)],
        out_specs=[rowb(D_MODEL), rowb(D_MODEL)],
        out_shape=[jax.ShapeDtypeStruct((t, D_MODEL), F32), jax.ShapeDtypeStruct((t, D_MODEL), BF16)],
        scratch_shapes=[
            pltpu.VMEM((SUBLANES, CONV_HALO + CONV_TL, D_MODEL), F32),
            pltpu.VMEM((CONV_TL, D_MODEL), F32),
        ],
        compiler_params=_params(("arbitrary", "arbitrary")),
        name="conv_merge_ln",
    )(x, proj, proj, y_n, gates, dww, dwb, lng, lnb, wconv, wssm, wout, g, b)


def _ffn_body(x_ref, xb_ref, wg_ref, wu_ref, wd_ref, g_ref, b_ref, o_ref, ob_ref):
    xb = xb_ref[...]
    hg = jnp.dot(xb, wg_ref[...], preferred_element_type=F32)
    hu = jnp.dot(xb, wu_ref[...], preferred_element_type=F32)
    f = jnp.dot((_silu(hg) * hu).astype(BF16), wd_ref[...], preferred_element_type=F32)
    out = _layer_norm(ALPHA * x_ref[...] + f, g_ref[...], b_ref[...])
    o_ref[...] = out
    ob_ref[...] = out.astype(BF16)


def _ffn(x, xb, wg, wu, wd, g, b, tm=512):
    t = x.shape[0]
    tm = min(tm, t)
    rowb = lambda w: pl.BlockSpec((tm, w), lambda m: (m, 0))
    full = lambda a: pl.BlockSpec(a.shape, lambda m: (0, 0), pipeline_mode=pl.Buffered(1))
    return pl.pallas_call(
        _ffn_body,
        grid=(t // tm,),
        in_specs=[rowb(D_MODEL), rowb(D_MODEL), full(wg), full(wu), full(wd), full(g), full(b)],
        out_specs=[rowb(D_MODEL), rowb(D_MODEL)],
        out_shape=[jax.ShapeDtypeStruct((t, D_MODEL), F32), jax.ShapeDtypeStruct((t, D_MODEL), BF16)],
        compiler_params=_params(("arbitrary",)),
        name="ffn_ln",
    )(x, xb, wg, wu, wd, g, b)


def _router_body(x_ref, wr_ref, wts_ref, meta_ref, cnt_ref, carry):
    @pl.when(pl.program_id(0) == 0)
    def _():
        carry[...] = jnp.zeros_like(carry)

    logits = jnp.dot(x_ref[...], wr_ref[...], preferred_element_type=F32, precision=lax.Precision.HIGHEST)
    tm = logits.shape[0]
    lane = lax.broadcasted_iota(jnp.int32, logits.shape, 1)
    logits = jnp.where(lane < N_EXPERTS, logits, -jnp.inf)
    m1 = jnp.max(logits, axis=-1, keepdims=True)
    i1 = jnp.min(jnp.where(logits == m1, lane, LANES), axis=-1, keepdims=True)
    rest = jnp.where(lane == i1, -jnp.inf, logits)
    m2 = jnp.max(rest, axis=-1, keepdims=True)
    i2 = jnp.min(jnp.where(rest == m2, lane, LANES), axis=-1, keepdims=True)
    e2 = jnp.exp(m2 - m1)
    denom = 1.0 + e2
    wts_ref[...] = jnp.where(lane == 0, 1.0 / denom, jnp.where(lane == 1, e2 / denom, 0.0))

    chosen = jnp.logical_or(lane == i1, lane == i2)
    chosen_f = jnp.where(chosen, 1.0, 0.0)
    r = lax.broadcasted_iota(jnp.int32, (tm, tm), 0)
    c = lax.broadcasted_iota(jnp.int32, (tm, tm), 1)
    earlier = jnp.where(c < r, 1.0, 0.0).astype(BF16)
    prefix = jnp.dot(earlier, chosen_f.astype(BF16), preferred_element_type=F32) + carry[...]
    rank1 = jnp.sum(jnp.where(lane == i1, prefix, 0.0), axis=-1, keepdims=True).astype(jnp.int32)
    rank2 = jnp.sum(jnp.where(lane == i2, prefix, 0.0), axis=-1, keepdims=True).astype(jnp.int32)
    meta = jnp.where(lane == 0, i1, jnp.where(lane == 1, i2,
                     jnp.where(lane == 2, rank1, jnp.where(lane == 3, rank2, 0))))
    meta_ref[...] = meta.T[0:SUBLANES, :]
    total = carry[...] + jnp.sum(chosen_f, axis=0, keepdims=True)
    carry[...] = total
    cnt_ref[...] = jnp.broadcast_to(total, cnt_ref.shape).astype(jnp.int32)


def _router(x, wr_pad, tm=1024):
    t = x.shape[0]
    tm = min(tm, t)
    return pl.pallas_call(
        _router_body,
        grid=(t // tm,),
        in_specs=[pl.BlockSpec((tm, D_MODEL), lambda m: (m, 0)),
                  pl.BlockSpec((D_MODEL, LANES), lambda m: (0, 0))],
        out_specs=[pl.BlockSpec((tm, LANES), lambda m: (m, 0)),
                   pl.BlockSpec((SUBLANES, tm), lambda m: (0, m)),
                   pl.BlockSpec((SUBLANES, LANES), lambda m: (0, 0))],
        out_shape=[jax.ShapeDtypeStruct((t, LANES), F32),
                   jax.ShapeDtypeStruct((SUBLANES, t), jnp.int32),
                   jax.ShapeDtypeStruct((SUBLANES, LANES), jnp.int32)],
        scratch_shapes=[pltpu.VMEM((1, LANES), F32)],
        compiler_params=_params(("arbitrary",)),
        name="router",
    )(x, wr_pad)


MOE_FF_SPLIT = 2
MOE_TM = 512
MOE_ROWS = 512
DMA_UNROLL = 8


def _row_copy(src, dst, src_row, dst_row, sem):
    return pltpu.make_async_copy(src.at[pl.ds(src_row, 1)], dst.at[pl.ds(dst_row, 1)], sem)


def _dispatch_body(slot1_ref, slot2_ref, x_ref, xs_in_hbm, xs_hbm, sem):
    del xs_in_hbm
    n = slot1_ref.shape[0]

    def issue(t, carry):
        _row_copy(x_ref, xs_hbm, t, slot1_ref[t], sem).start()
        _row_copy(x_ref, xs_hbm, t, slot2_ref[t], sem).start()
        return carry

    lax.fori_loop(0, n, issue, 0, unroll=DMA_UNROLL)

    def drain(t, carry):
        _row_copy(x_ref, xs_hbm, 0, 0, sem).wait()
        _row_copy(x_ref, xs_hbm, 0, 0, sem).wait()
        return carry

    lax.fori_loop(0, n, drain, 0, unroll=DMA_UNROLL)


def _dispatch(x, slot1, slot2, xs_zero):
    t = x.shape[0]
    n = min(MOE_ROWS, t)
    smem = lambda: pl.BlockSpec((n,), lambda i: (i,), memory_space=pltpu.SMEM)
    return pl.pallas_call(
        _dispatch_body,
        grid=(t // n,),
        in_specs=[smem(), smem(), pl.BlockSpec((n, D_MODEL), lambda i: (i, 0)),
                  pl.BlockSpec(memory_space=pl.ANY)],
        out_specs=pl.BlockSpec(memory_space=pl.ANY),
        out_shape=jax.ShapeDtypeStruct(xs_zero.shape, xs_zero.dtype),
        scratch_shapes=[pltpu.SemaphoreType.DMA(())],
        input_output_aliases={3: 0},
        compiler_params=_params(("arbitrary",)),
        name="moe_dispatch",
    )(slot1, slot2, x, xs_zero)


def _experts_body(te_ref, valid_ref, xs_ref, wg_ref, wu_ref, wd_ref, o_ref, xb):
    j = pl.program_id(0)
    f = pl.program_id(1)

    @pl.when(f == 0)
    def _():
        xb[...] = xs_ref[...].astype(BF16)
        o_ref[...] = jnp.zeros_like(o_ref)

    @pl.when(valid_ref[j] != 0)
    def _():
        hg = jnp.dot(xb[...], wg_ref[0], preferred_element_type=F32)
        hu = jnp.dot(xb[...], wu_ref[0], preferred_element_type=F32)
        o_ref[...] += jnp.dot((_silu(hg) * hu).astype(BF16), wd_ref[0], preferred_element_type=F32)


def _experts(xs, tile_expert, tile_valid, wg, wu, wd):
    rows = xs.shape[0]
    tm = MOE_TM
    ffs = wg.shape[-1] // MOE_FF_SPLIT
    grid_spec = pltpu.PrefetchScalarGridSpec(
        num_scalar_prefetch=2,
        grid=(rows // tm, MOE_FF_SPLIT),
        in_specs=[
            pl.BlockSpec((tm, D_MODEL), lambda j, f, te, va: (j, 0)),
            pl.BlockSpec((1, D_MODEL, ffs), lambda j, f, te, va: (te[j], 0, f)),
            pl.BlockSpec((1, D_MODEL, ffs), lambda j, f, te, va: (te[j], 0, f)),
            pl.BlockSpec((1, ffs, D_MODEL), lambda j, f, te, va: (te[j], f, 0)),
        ],
        out_specs=pl.BlockSpec((tm, D_MODEL), lambda j, f, te, va: (j, 0)),
        scratch_shapes=[pltpu.VMEM((tm, D_MODEL), BF16)],
    )
    return pl.pallas_call(
        _experts_body,
        grid_spec=grid_spec,
        out_shape=jax.ShapeDtypeStruct((rows, D_MODEL), F32),
        compiler_params=_params(("arbitrary", "arbitrary")),
        name="moe_experts",
    )(tile_expert, tile_valid, xs, wg, wu, wd)


def _combine_body(slot1_ref, slot2_ref, x_ref, wts_ref, g_ref, b_ref, y_hbm, o_ref, y1, y2, sem):
    n = slot1_ref.shape[0]

    def issue(t, carry):
        _row_copy(y_hbm, y1, slot1_ref[t], t, sem).start()
        _row_copy(y_hbm, y2, slot2_ref[t], t, sem).start()
        return carry

    lax.fori_loop(0, n, issue, 0, unroll=DMA_UNROLL)

    def drain(t, carry):
        _row_copy(y_hbm, y1, 0, 0, sem).wait()
        _row_copy(y_hbm, y2, 0, 0, sem).wait()
        return carry

    lax.fori_loop(0, n, drain, 0, unroll=DMA_UNROLL)

    w = wts_ref[...]
    f = w[:, 0:1] * y1[...] + w[:, 1:2] * y2[...]
    o_ref[...] = _layer_norm(ALPHA * x_ref[...] + f, g_ref[...], b_ref[...])


def _combine(x, ys, slot1, slot2, wts, g, b):
    t = x.shape[0]
    n = min(MOE_ROWS, t)
    smem = lambda: pl.BlockSpec((n,), lambda i: (i,), memory_space=pltpu.SMEM)
    rowb = lambda w: pl.BlockSpec((n, w), lambda i: (i, 0))
    full = lambda a: pl.BlockSpec(a.shape, lambda i: (0, 0))
    return pl.pallas_call(
        _combine_body,
        grid=(t // n,),
        in_specs=[smem(), smem(), rowb(D_MODEL), rowb(LANES), full(g), full(b),
                  pl.BlockSpec(memory_space=pl.ANY)],
        out_specs=rowb(D_MODEL),
        out_shape=jax.ShapeDtypeStruct((t, D_MODEL), F32),
        scratch_shapes=[pltpu.VMEM((n, D_MODEL), F32), pltpu.VMEM((n, D_MODEL), F32),
                        pltpu.SemaphoreType.DMA(())],
        compiler_params=_params(("arbitrary",)),
        name="moe_combine_ln",
    )(slot1, slot2, x, wts, g, b, ys)


def _moe(x, wr_pad, wg, wu, wd, g, b):
    t = x.shape[0]
    wts, meta, cnt = _router(x, wr_pad)
    counts = cnt[0, :N_EXPERTS]
    padded = ((counts + MOE_TM - 1) // MOE_TM) * MOE_TM
    seg_end = jnp.cumsum(padded)
    seg_off = seg_end - padded
    def slot(idx, rank):
        for e in range(N_EXPERTS):
            rank = rank + jnp.where(idx == e, seg_off[e], 0)
        return rank.astype(jnp.int32)

    slot1 = slot(meta[0], meta[2])
    slot2 = slot(meta[1], meta[3])
    n_tiles = (2 * t) // MOE_TM + N_EXPERTS
    tile_start = jnp.arange(n_tiles, dtype=jnp.int32) * MOE_TM
    tile_expert = jnp.minimum(jnp.sum(tile_start[:, None] >= seg_end[None, :], axis=1), N_EXPERTS - 1)
    tile_valid = (tile_start < seg_end[-1]).astype(jnp.int32)

    xs = _dispatch(x, slot1, slot2, jnp.zeros((n_tiles * MOE_TM, D_MODEL), F32))
    ys = _experts(xs, tile_expert.astype(jnp.int32), tile_valid, wg, wu, wd)
    return _combine(x, ys, slot1, slot2, wts, g, b)


def _pad_lanes(v):
    return jnp.pad(v, ((0, 0), (0, LANES - v.shape[-1])))


def _mixer(x, xb, bsz, seqlen, w_in_all, layer, conv_dw_w, conv_dw_b, conv_ln_g, conv_ln_b, conv_w_out,
           ssm_conv_w, ssm_conv_b, ssm_dt_bias, ssm_a_log, ssm_d, ssm_norm_w, ssm_w_out, w_out, ln_g, ln_b):
    proj = _matmul(xb, w_in_all, layer=layer, n_blocks=MAIN_W // 1024, tn=1024, out_dtype=BF16)
    dt_raw = _matmul(xb, _pad_lanes(w_in_all[layer, :, COL_DT:COL_GATES]), n_blocks=1, tn=LANES, out_dtype=F32)
    gates = _matmul(xb, w_in_all[layer, :, COL_GATES:], n_blocks=2, tn=1024, out_dtype=BF16)
    y_n = _ssd_branch(proj, dt_raw, ssm_conv_w, ssm_conv_b[None], _pad_lanes(ssm_dt_bias[None]),
                      _pad_lanes(ssm_a_log[None]), jnp.repeat(ssm_d, HEADDIM)[None], ssm_norm_w[None],
                      bsz, seqlen)
    return _mix(x, proj, y_n, gates, conv_dw_w, conv_dw_b[None], conv_ln_g[None], conv_ln_b[None],
                conv_w_out.astype(BF16), ssm_w_out.astype(BF16), w_out.astype(BF16), ln_g[None], ln_b[None],
                bsz, seqlen)


def kernel(x, mix_w_in, conv_dw_w, conv_dw_b, conv_ln_g, conv_ln_b, conv_w_out, ssm_conv_w, ssm_conv_b,
           ssm_dt_bias, ssm_a_log, ssm_d, ssm_norm_w, ssm_w_out, mix_w_out, ln_mix_g, ln_mix_b,
           ffn_w_gate, ffn_w_up, ffn_w_down, moe_router, moe_w_gate, moe_w_up, moe_w_down,
           ln_ffn_g, ln_ffn_b):
    bsz, seqlen, d = x.shape
    xf = x.reshape(bsz * seqlen, d)
    xb = xf
    for i in range(DEPTH):
        xf, xb = _mixer(xf, xb, bsz, seqlen, mix_w_in, i, conv_dw_w[i], conv_dw_b[i], conv_ln_g[i],
                        conv_ln_b[i], conv_w_out[i], ssm_conv_w[i], ssm_conv_b[i], ssm_dt_bias[i],
                        ssm_a_log[i], ssm_d[i], ssm_norm_w[i], ssm_w_out[i], mix_w_out[i],
                        ln_mix_g[i], ln_mix_b[i])
        j = i // 2
        if i % 2 == 0:
            xf, xb = _ffn(xf, xb, ffn_w_gate[j].astype(BF16), ffn_w_up[j].astype(BF16),
                          ffn_w_down[j].astype(BF16), ln_ffn_g[i][None], ln_ffn_b[i][None])
        else:
            xf = _moe(xf, _pad_lanes(moe_router[j]), moe_w_gate[j].astype(BF16), moe_w_up[j].astype(BF16),
                      moe_w_down[j].astype(BF16), ln_ffn_g[i][None], ln_ffn_b[i][None])
            xb = xf.astype(BF16)
    return xf.reshape(bsz, seqlen, d)
```

```python
import functools
import math

import jax
import jax.numpy as jnp
from jax import lax
from jax.experimental import pallas as pl
from jax.experimental.pallas import tpu as pltpu

F32 = jnp.float32
BF16 = jnp.bfloat16

D_MODEL = 1024
DEPTH = 2
CONV_K = 31
D_INNER = 2048
HEADDIM = 64
HEADS = D_INNER // HEADDIM
GROUPS = 4
HEADS_PER_GROUP = HEADS // GROUPS
STATE = 128
SSM_K = 4
CHUNK = 128
GROUP_W = D_INNER // GROUPS
BC_W = 2 * GROUPS * STATE
N_EXPERTS = 8
LN_EPS = 1e-5
RMS_EPS = 1e-5
ALPHA = (2 * DEPTH) ** 0.25

COL_Z = 2 * D_MODEL
COL_XS = COL_Z + D_INNER
COL_BC = COL_XS + D_INNER
COL_DT = COL_BC + BC_W
COL_GATES = COL_DT + HEADS
MAIN_W = COL_DT

LANES = 128
SUBLANES = 8
VMEM_LIMIT = 56 * 1024 * 1024


def _params(sem):
    return pltpu.CompilerParams(dimension_semantics=sem, vmem_limit_bytes=VMEM_LIMIT)


def _layer_norm(v, g, b):
    mu = jnp.mean(v, axis=-1, keepdims=True)
    d = v - mu
    var = jnp.mean(d * d, axis=-1, keepdims=True)
    return d * lax.rsqrt(var + LN_EPS) * g + b


def _silu(v):
    half = 0.5 * v
    return half + half * jnp.tanh(half)


def _mm_body(x_ref, w_ref, o_ref, wb):
    @pl.when(pl.program_id(1) == 0)
    def _():
        wb[...] = w_ref[...].astype(BF16)

    o_ref[...] = jnp.dot(x_ref[...].astype(BF16), wb[...], preferred_element_type=F32).astype(o_ref.dtype)


def _matmul(x, w, *, n_blocks, tn, out_dtype, layer=None, tm=1024):
    t, k = x.shape
    tm = min(tm, t)
    if layer is None:
        w_spec = pl.BlockSpec((k, tn), lambda n, m: (0, n))
    else:
        w = w.reshape(w.shape[0] * k, w.shape[2])
        w_spec = pl.BlockSpec((k, tn), lambda n, m: (layer, n))
    return pl.pallas_call(
        _mm_body,
        grid=(n_blocks, t // tm),
        in_specs=[
            pl.BlockSpec((tm, k), lambda n, m: (m, 0)),
            w_spec,
        ],
        out_specs=pl.BlockSpec((tm, tn), lambda n, m: (m, n)),
        out_shape=jax.ShapeDtypeStruct((t, n_blocks * tn), out_dtype),
        scratch_shapes=[pltpu.VMEM((k, tn), BF16)],
        compiler_params=_params(("arbitrary", "arbitrary")),
        name="proj_matmul",
    )(x, w)


SSD_LANE_BLK = 512
SSD_HALO = 16


def _conv4_silu(cur_ref, ext, w_ref, b_ref, c, store):
    width = cur_ref.shape[1]

    @pl.when(c == 0)
    def _():
        ext[0:SSD_HALO, :] = jnp.zeros((SSD_HALO, width), BF16)

    @pl.when(c > 0)
    def _():
        ext[0:SSD_HALO, :] = ext[CHUNK:CHUNK + SSD_HALO, :]

    ext[SSD_HALO:SSD_HALO + CHUNK, :] = cur_ref[...]
    n_shift = SSM_K - 1
    r = lax.broadcasted_iota(jnp.int32, (n_shift * CHUNK, SSD_HALO + CHUNK), 0)
    col = lax.broadcasted_iota(jnp.int32, (n_shift * CHUNK, SSD_HALO + CHUNK), 1)
    sel = jnp.zeros(r.shape, F32)
    for k in range(n_shift):
        in_rows = jnp.logical_and(r >= k * CHUNK, r < (k + 1) * CHUNK)
        hit = jnp.logical_and(in_rows, col == r - k * CHUNK + SSD_HALO - n_shift + k)
        sel = jnp.where(hit, 1.0, sel)
    sel = sel.astype(BF16)
    for i in range(width // SSD_LANE_BLK):
        ls = slice(i * SSD_LANE_BLK, (i + 1) * SSD_LANE_BLK)
        shifted = jnp.dot(sel, ext[:, ls], preferred_element_type=F32)
        out = b_ref[:, ls] + w_ref[n_shift:n_shift + 1, ls] * ext[SSD_HALO:SSD_HALO + CHUNK, ls].astype(F32)
        for k in range(n_shift):
            out = out + w_ref[k:k + 1, ls] * shifted[k * CHUNK:(k + 1) * CHUNK, :]
        store(ls, _silu(out))


def _ssd_body(xs_ref, bc_ref, z_ref, dt_ref, cwx_ref, cbx_ref, cwbc_ref, cbbc_ref, dtb_ref, alog_ref,
              dvec_ref, nw_ref, o_ref, ext_x, ext_bc, xs_f, xs_b, bc_f, state, ybuf):
    c = pl.program_id(1)

    @pl.when(c == 0)
    def _():
        state[...] = jnp.zeros_like(state)

    def store_xs(ls, v):
        xs_f[:, ls] = v
        xs_b[:, ls] = v.astype(BF16)

    def store_bc(ls, v):
        bc_f[:, ls] = v

    _conv4_silu(xs_ref, ext_x, cwx_ref, cbx_ref, c, store_xs)
    _conv4_silu(bc_ref, ext_bc, cwbc_ref, cbbc_ref, c, store_bc)

    dt_raw = dt_ref[...] + dtb_ref[...]
    dt = jnp.maximum(dt_raw, 0.0) + jnp.log1p(jnp.exp(-jnp.abs(dt_raw)))
    da = dt * (-jnp.exp(alog_ref[...]))
    row = lax.broadcasted_iota(jnp.int32, (CHUNK, CHUNK), 0)
    lane = lax.broadcasted_iota(jnp.int32, (CHUNK, CHUNK), 1)
    causal = row >= lane
    first_head = lane < HEADDIM
    acum = jnp.dot(causal.astype(F32), da, preferred_element_type=F32,
                   precision=lax.Precision.HIGHEST)
    eacum = jnp.exp(acum)
    acum_t = acum.T
    dt_t = dt.T
    src_t = acum_t - jnp.log(dt_t)
    last = acum_t[:, CHUNK - 1:CHUNK]
    wt_t = jnp.exp(last - acum_t) * dt_t
    chunk_decay = eacum[CHUNK - 1:CHUNK, :]

    def block_diag(v):
        zero = jnp.zeros_like(v)
        return jnp.concatenate([jnp.where(first_head, v, zero), jnp.where(first_head, zero, v)], axis=0)

    for g in range(GROUPS):
        bg = bc_f[:, g * STATE:(g + 1) * STATE]
        cg = bc_f[:, GROUPS * STATE + g * STATE:GROUPS * STATE + (g + 1) * STATE]
        cb = lax.dot_general(cg.astype(BF16), bg.astype(BF16), (((1,), (1,)), ((), ())),
                             preferred_element_type=F32)
        bg_t = bg.T
        for q in range(HEADS_PER_GROUP // 2):
            pair = (g * HEADS_PER_GROUP) // 2 + q
            ps = slice(pair * 2 * HEADDIM, (pair + 1) * 2 * HEADDIM)
            x_bd = block_diag(xs_b[:, ps])
            prev = state[pair]
            s_bd = block_diag(prev.astype(BF16))
            m_parts, ce_parts, bw_parts = [], [], []
            for h in (2 * pair, 2 * pair + 1):
                colb = jnp.broadcast_to(acum[:, h:h + 1], (CHUNK, CHUNK))
                ecolb = jnp.broadcast_to(eacum[:, h:h + 1], (CHUNK, CHUNK))
                rowb = jnp.broadcast_to(src_t[h:h + 1, :], (CHUNK, CHUNK))
                m_parts.append((cb * jnp.exp(jnp.where(causal, colb - rowb, -jnp.inf))).astype(BF16))
                ce_parts.append((cg * ecolb).astype(BF16))
                bw_parts.append((bg_t * wt_t[h:h + 1, :]).astype(BF16))
            y2 = jnp.dot(jnp.concatenate(m_parts + ce_parts, axis=1), jnp.concatenate([x_bd, s_bd], axis=0),
                         preferred_element_type=F32)
            s2 = jnp.dot(jnp.concatenate(bw_parts, axis=1), x_bd, preferred_element_type=F32)
            h0 = 2 * pair
            cd2 = jnp.where(first_head[0:1, :], chunk_decay[:, h0:h0 + 1], chunk_decay[:, h0 + 1:h0 + 2])
            state[pair] = prev * cd2 + s2
            ybuf[:, ps] = y2

    for g in range(GROUPS):
        gs = slice(g * GROUP_W, (g + 1) * GROUP_W)
        y = ybuf[:, gs] + dvec_ref[:, gs] * xs_f[:, gs]
        hg = y * _silu(z_ref[:, gs].astype(F32))
        ms = jnp.mean(hg * hg, axis=-1, keepdims=True)
        o_ref[:, gs] = (hg * lax.rsqrt(ms + RMS_EPS) * nw_ref[:, gs]).astype(o_ref.dtype)


def _ssd_branch(proj, dt_raw, cw, cb, dtb, alog, dvec, nw, bsz, seqlen):
    nc = seqlen // CHUNK
    row = lambda b, c: b * nc + c
    const = lambda b, c: (0, 0)
    xs_blk = COL_XS // D_INNER
    bc_blk = COL_BC // BC_W
    z_blk = COL_Z // D_INNER
    return pl.pallas_call(
        _ssd_body,
        grid=(bsz, nc),
        in_specs=[
            pl.BlockSpec((CHUNK, D_INNER), lambda b, c: (row(b, c), xs_blk)),
            pl.BlockSpec((CHUNK, BC_W), lambda b, c: (row(b, c), bc_blk)),
            pl.BlockSpec((CHUNK, D_INNER), lambda b, c: (row(b, c), z_blk)),
            pl.BlockSpec((CHUNK, LANES), lambda b, c: (row(b, c), 0)),
            pl.BlockSpec((SSM_K, D_INNER), lambda b, c: (0, 0)),
            pl.BlockSpec((1, D_INNER), lambda b, c: (0, 0)),
            pl.BlockSpec((SSM_K, BC_W), lambda b, c: (0, D_INNER // BC_W)),
            pl.BlockSpec((1, BC_W), lambda b, c: (0, D_INNER // BC_W)),
            pl.BlockSpec((1, LANES), const),
            pl.BlockSpec((1, LANES), const),
            pl.BlockSpec((1, D_INNER), const),
            pl.BlockSpec((1, D_INNER), const),
        ],
        out_specs=pl.BlockSpec((CHUNK, D_INNER), lambda b, c: (row(b, c), 0)),
        out_shape=jax.ShapeDtypeStruct((bsz * seqlen, D_INNER), BF16),
        scratch_shapes=[
            pltpu.VMEM((SSD_HALO + CHUNK, D_INNER), BF16),
            pltpu.VMEM((SSD_HALO + CHUNK, BC_W), BF16),
            pltpu.VMEM((CHUNK, D_INNER), F32),
            pltpu.VMEM((CHUNK, D_INNER), BF16),
            pltpu.VMEM((CHUNK, BC_W), F32),
            pltpu.VMEM((HEADS // 2, STATE, 2 * HEADDIM), F32),
            pltpu.VMEM((CHUNK, D_INNER), F32),
        ],
        compiler_params=_params(("arbitrary", "arbitrary")),
        name="ssd_branch",
    )(proj, proj, proj, dt_raw, cw, cb, cw, cb, dtb, alog, dvec, nw)


CONV_TL = 256
CONV_HALO = 32
CONV_RB = 64


def _mix_body(x_ref, cv_ref, cg_ref, yn_ref, gt_ref, dww_ref, dwb_ref, lng_ref, lnb_ref, wconv_ref,
              wssm_ref, wout_ref, g_ref, b_ref, o_ref, ob_ref, ubuf, cbuf):
    l = pl.program_id(1)

    @pl.when(l == 0)
    def _():
        ubuf[0, 0:CONV_HALO, :] = jnp.zeros((CONV_HALO, D_MODEL), F32)

    @pl.when(l > 0)
    def _():
        ubuf[0, 0:CONV_HALO, :] = ubuf[0, CONV_TL:CONV_TL + CONV_HALO, :]

    y_ssm = jnp.dot(yn_ref[...], wssm_ref[...], preferred_element_type=F32)

    cv = cv_ref[...].astype(F32)
    cg = cg_ref[...].astype(F32)
    ubuf[0, CONV_HALO:CONV_HALO + CONV_TL, :] = cv * jax.nn.sigmoid(cg)

    nshift = CONV_HALO + CONV_TL - SUBLANES
    for c in range(D_MODEL // LANES):
        cs = slice(c * LANES, (c + 1) * LANES)
        for j in range(1, SUBLANES):
            ubuf[j, 0:nshift, cs] = ubuf[0, j:j + nshift, cs]

    base = CONV_HALO - (CONV_K - 1)
    for c in range(D_MODEL // LANES):
        cs = slice(c * LANES, (c + 1) * LANES)
        wc = dww_ref[:, cs]
        bc = dwb_ref[:, cs]
        for r in range(CONV_TL // CONV_RB):
            acc = jnp.broadcast_to(bc, (CONV_RB, LANES))
            for k in range(CONV_K):
                j = (base + k) % SUBLANES
                r0 = base + k - j + r * CONV_RB
                acc = acc + wc[k:k + 1, :] * ubuf[j, r0:r0 + CONV_RB, cs]
            cbuf[r * CONV_RB:(r + 1) * CONV_RB, cs] = acc

    v = _layer_norm(cbuf[...], lng_ref[...], lnb_ref[...])
    y_conv = jnp.dot(_silu(v).astype(BF16), wconv_ref[...], preferred_element_type=F32)

    gates = gt_ref[...].astype(F32)
    hmix = jax.nn.sigmoid(gates[:, :D_MODEL]) * y_conv + jax.nn.sigmoid(gates[:, D_MODEL:]) * y_ssm
    mix = jnp.dot(hmix.astype(BF16), wout_ref[...], preferred_element_type=F32)
    out = _layer_norm(ALPHA * x_ref[...] + mix, g_ref[...], b_ref[...])
    o_ref[...] = out
    ob_ref[...] = out.astype(BF16)


def _mix(x, proj, y_n, gates, dww, dwb, lng, lnb, wconv, wssm, wout, g, b, bsz, seqlen):
    nl = seqlen // CONV_TL
    t = bsz * seqlen
    rowb = lambda w, col=0: pl.BlockSpec((CONV_TL, w), lambda bb, l: (bb * nl + l, col))
    full = lambda a: pl.BlockSpec(a.shape, lambda bb, l: (0, 0), pipeline_mode=pl.Buffered(1))
    return pl.pallas_call(
        _mix_body,
        grid=(bsz, nl),
        in_specs=[rowb(D_MODEL), rowb(D_MODEL, 0), rowb(D_MODEL, 1), rowb(D_INNER), rowb(2 * D_MODEL),
                  full(dww), full(dwb), full(lng), full(lnb), full(wconv), full(wssm), full(wout),
                  full(g), full(b)],
        out_specs=[rowb(D_MODEL), rowb(D_MODEL)],
        out_shape=[jax.ShapeDtypeStruct((t, D_MODEL), F32), jax.ShapeDtypeStruct((t, D_MODEL), BF16)],
        scratch_shapes=[
            pltpu.VMEM((SUBLANES, CONV_HALO + CONV_TL, D_MODEL), F32),
            pltpu.VMEM((CONV_TL, D_MODEL), F32),
        ],
        compiler_params=_params(("arbitrary", "arbitrary")),
        name="conv_merge_ln",
    )(x, proj, proj, y_n, gates, dww, dwb, lng, lnb, wconv, wssm, wout, g, b)


def _ffn_body(x_ref, xb_ref, wg_ref, wu_ref, wd_ref, g_ref, b_ref, o_ref, ob_ref):
    xb = xb_ref[...]
    hg = jnp.dot(xb, wg_ref[...], preferred_element_type=F32)
    hu = jnp.dot(xb, wu_ref[...], preferred_element_type=F32)
    f = jnp.dot((_silu(hg) * hu).astype(BF16), wd_ref[...], preferred_element_type=F32)
    out = _layer_norm(ALPHA * x_ref[...] + f, g_ref[...], b_ref[...])
    o_ref[...] = out
    ob_ref[...] = out.astype(BF16)


def _ffn(x, xb, wg, wu, wd, g, b, tm=512):
    t = x.shape[0]
    tm = min(tm, t)
    rowb = lambda w: pl.BlockSpec((tm, w), lambda m: (m, 0))
    full = lambda a: pl.BlockSpec(a.shape, lambda m: (0, 0), pipeline_mode=pl.Buffered(1))
    return pl.pallas_call(
        _ffn_body,
        grid=(t // tm,),
        in_specs=[rowb(D_MODEL), rowb(D_MODEL), full(wg), full(wu), full(wd), full(g), full(b)],
        out_specs=[rowb(D_MODEL), rowb(D_MODEL)],
        out_shape=[jax.ShapeDtypeStruct((t, D_MODEL), F32), jax.ShapeDtypeStruct((t, D_MODEL), BF16)],
        compiler_params=_params(("arbitrary",)),
        name="ffn_ln",
    )(x, xb, wg, wu, wd, g, b)


def _router_body(x_ref, wr_ref, wts_ref, meta_ref, cnt_ref, carry):
    @pl.when(pl.program_id(0) == 0)
    def _():
        carry[...] = jnp.zeros_like(carry)

    logits = jnp.dot(x_ref[...], wr_ref[...], preferred_element_type=F32, precision=lax.Precision.HIGHEST)
    tm = logits.shape[0]
    lane = lax.broadcasted_iota(jnp.int32, logits.shape, 1)
    logits = jnp.where(lane < N_EXPERTS, logits, -jnp.inf)
    m1 = jnp.max(logits, axis=-1, keepdims=True)
    i1 = jnp.min(jnp.where(logits == m1, lane, LANES), axis=-1, keepdims=True)
    rest = jnp.where(lane == i1, -jnp.inf, logits)
    m2 = jnp.max(rest, axis=-1, keepdims=True)
    i2 = jnp.min(jnp.where(rest == m2, lane, LANES), axis=-1, keepdims=True)
    e2 = jnp.exp(m2 - m1)
    denom = 1.0 + e2
    wts_ref[...] = jnp.where(lane == 0, 1.0 / denom, jnp.where(lane == 1, e2 / denom, 0.0))

    chosen = jnp.logical_or(lane == i1, lane == i2)
    chosen_f = jnp.where(chosen, 1.0, 0.0)
    r = lax.broadcasted_iota(jnp.int32, (tm, tm), 0)
    c = lax.broadcasted_iota(jnp.int32, (tm, tm), 1)
    earlier = jnp.where(c < r, 1.0, 0.0).astype(BF16)
    prefix = jnp.dot(earlier, chosen_f.astype(BF16), preferred_element_type=F32) + carry[...]
    rank1 = jnp.sum(jnp.where(lane == i1, prefix, 0.0), axis=-1, keepdims=True).astype(jnp.int32)
    rank2 = jnp.sum(jnp.where(lane == i2, prefix, 0.0), axis=-1, keepdims=True).astype(jnp.int32)
    meta = jnp.where(lane == 0, i1, jnp.where(lane == 1, i2,
                     jnp.where(lane == 2, rank1, jnp.where(lane == 3, rank2, 0))))
    meta_ref[...] = meta.T[0:SUBLANES, :]
    total = carry[...] + jnp.sum(chosen_f, axis=0, keepdims=True)
    carry[...] = total
    cnt_ref[...] = jnp.broadcast_to(total, cnt_ref.shape).astype(jnp.int32)


def _router(x, wr_pad, tm=1024):
    t = x.shape[0]
    tm = min(tm, t)
    return pl.pallas_call(
        _router_body,
        grid=(t // tm,),
        in_specs=[pl.BlockSpec((tm, D_MODEL), lambda m: (m, 0)),
                  pl.BlockSpec((D_MODEL, LANES), lambda m: (0, 0))],
        out_specs=[pl.BlockSpec((tm, LANES), lambda m: (m, 0)),
                   pl.BlockSpec((SUBLANES, tm), lambda m: (0, m)),
                   pl.BlockSpec((SUBLANES, LANES), lambda m: (0, 0))],
        out_shape=[jax.ShapeDtypeStruct((t, LANES), F32),
                   jax.ShapeDtypeStruct((SUBLANES, t), jnp.int32),
                   jax.ShapeDtypeStruct((SUBLANES, LANES), jnp.int32)],
        scratch_shapes=[pltpu.VMEM((1, LANES), F32)],
        compiler_params=_params(("arbitrary",)),
        name="router",
    )(x, wr_pad)


MOE_FF_SPLIT = 2
MOE_TM = 512
MOE_ROWS = 512
DMA_UNROLL = 8


def _row_copy(src, dst, src_row, dst_row, sem):
    return pltpu.make_async_copy(src.at[pl.ds(src_row, 1)], dst.at[pl.ds(dst_row, 1)], sem)


def _dispatch_body(slot1_ref, slot2_ref, x_ref, xs_in_hbm, xs_hbm, sem):
    del xs_in_hbm
    n = slot1_ref.shape[0]

    def issue(t, carry):
        _row_copy(x_ref, xs_hbm, t, slot1_ref[t], sem).start()
        _row_copy(x_ref, xs_hbm, t, slot2_ref[t], sem).start()
        return carry

    lax.fori_loop(0, n, issue, 0, unroll=DMA_UNROLL)

    def drain(t, carry):
        _row_copy(x_ref, xs_hbm, 0, 0, sem).wait()
        _row_copy(x_ref, xs_hbm, 0, 0, sem).wait()
        return carry

    lax.fori_loop(0, n, drain, 0, unroll=DMA_UNROLL)


def _dispatch(x, slot1, slot2, xs_zero):
    t = x.shape[0]
    n = min(MOE_ROWS, t)
    smem = lambda: pl.BlockSpec((n,), lambda i: (i,), memory_space=pltpu.SMEM)
    return pl.pallas_call(
        _dispatch_body,
        grid=(t // n,),
        in_specs=[smem(), smem(), pl.BlockSpec((n, D_MODEL), lambda i: (i, 0)),
                  pl.BlockSpec(memory_space=pl.ANY)],
        out_specs=pl.BlockSpec(memory_space=pl.ANY),
        out_shape=jax.ShapeDtypeStruct(xs_zero.shape, xs_zero.dtype),
        scratch_shapes=[pltpu.SemaphoreType.DMA(())],
        input_output_aliases={3: 0},
        compiler_params=_params(("arbitrary",)),
        name="moe_dispatch",
    )(slot1, slot2, x, xs_zero)


def _experts_body(te_ref, valid_ref, xs_ref, wg_ref, wu_ref, wd_ref, o_ref, xb):
    j = pl.program_id(0)
    f = pl.program_id(1)

    @pl.when(f == 0)
    def _():
        xb[...] = xs_ref[...].astype(BF16)
        o_ref[...] = jnp.zeros_like(o_ref)

    @pl.when(valid_ref[j] != 0)
    def _():
        hg = jnp.dot(xb[...], wg_ref[0], preferred_element_type=F32)
        hu = jnp.dot(xb[...], wu_ref[0], preferred_element_type=F32)
        o_ref[...] += jnp.dot((_silu(hg) * hu).astype(BF16), wd_ref[0], preferred_element_type=F32)


def _experts(xs, tile_expert, tile_valid, wg, wu, wd):
    rows = xs.shape[0]
    tm = MOE_TM
    ffs = wg.shape[-1] // MOE_FF_SPLIT
    grid_spec = pltpu.PrefetchScalarGridSpec(
        num_scalar_prefetch=2,
        grid=(rows // tm, MOE_FF_SPLIT),
        in_specs=[
            pl.BlockSpec((tm, D_MODEL), lambda j, f, te, va: (j, 0)),
            pl.BlockSpec((1, D_MODEL, ffs), lambda j, f, te, va: (te[j], 0, f)),
            pl.BlockSpec((1, D_MODEL, ffs), lambda j, f, te, va: (te[j], 0, f)),
            pl.BlockSpec((1, ffs, D_MODEL), lambda j, f, te, va: (te[j], f, 0)),
        ],
        out_specs=pl.BlockSpec((tm, D_MODEL), lambda j, f, te, va: (j, 0)),
        scratch_shapes=[pltpu.VMEM((tm, D_MODEL), BF16)],
    )
    return pl.pallas_call(
        _experts_body,
        grid_spec=grid_spec,
        out_shape=jax.ShapeDtypeStruct((rows, D_MODEL), F32),
        compiler_params=_params(("arbitrary", "arbitrary")),
        name="moe_experts",
    )(tile_expert, tile_valid, xs, wg, wu, wd)


def _combine_body(slot1_ref, slot2_ref, x_ref, wts_ref, g_ref, b_ref, y_hbm, o_ref, y1, y2, sem):
    n = slot1_ref.shape[0]

    def issue(t, carry):
        _row_copy(y_hbm, y1, slot1_ref[t], t, sem).start()
        _row_copy(y_hbm, y2, slot2_ref[t], t, sem).start()
        return carry

    lax.fori_loop(0, n, issue, 0, unroll=DMA_UNROLL)

    def drain(t, carry):
        _row_copy(y_hbm, y1, 0, 0, sem).wait()
        _row_copy(y_hbm, y2, 0, 0, sem).wait()
        return carry

    lax.fori_loop(0, n, drain, 0, unroll=DMA_UNROLL)

    w = wts_ref[...]
    f = w[:, 0:1] * y1[...] + w[:, 1:2] * y2[...]
    o_ref[...] = _layer_norm(ALPHA * x_ref[...] + f, g_ref[...], b_ref[...])


def _combine(x, ys, slot1, slot2, wts, g, b):
    t = x.shape[0]
    n = min(MOE_ROWS, t)
    smem = lambda: pl.BlockSpec((n,), lambda i: (i,), memory_space=pltpu.SMEM)
    rowb = lambda w: pl.BlockSpec((n, w), lambda i: (i, 0))
    full = lambda a: pl.BlockSpec(a.shape, lambda i: (0, 0))
    return pl.pallas_call(
        _combine_body,
        grid=(t // n,),
        in_specs=[smem(), smem(), rowb(D_MODEL), rowb(LANES), full(g), full(b),
                  pl.BlockSpec(memory_space=pl.ANY)],
        out_specs=rowb(D_MODEL),
        out_shape=jax.ShapeDtypeStruct((t, D_MODEL), F32),
        scratch_shapes=[pltpu.VMEM((n, D_MODEL), F32), pltpu.VMEM((n, D_MODEL), F32),
                        pltpu.SemaphoreType.DMA(())],
        compiler_params=_params(("arbitrary",)),
        name="moe_combine_ln",
    )(slot1, slot2, x, wts, g, b, ys)


def _moe(x, wr_pad, wg, wu, wd, g, b):
    t = x.shape[0]
    wts, meta, cnt = _router(x, wr_pad)
    counts = cnt[0, :N_EXPERTS]
    padded = ((counts + MOE_TM - 1) // MOE_TM) * MOE_TM
    seg_end = jnp.cumsum(padded)
    seg_off = seg_end - padded
    def slot(idx, rank):
        for e in range(N_EXPERTS):
            rank = rank + jnp.where(idx == e, seg_off[e], 0)
        return rank.astype(jnp.int32)

    slot1 = slot(meta[0], meta[2])
    slot2 = slot(meta[1], meta[3])
    n_tiles = (2 * t) // MOE_TM + N_EXPERTS
    tile_start = jnp.arange(n_tiles, dtype=jnp.int32) * MOE_TM
    tile_expert = jnp.minimum(jnp.sum(tile_start[:, None] >= seg_end[None, :], axis=1), N_EXPERTS - 1)
    tile_valid = (tile_start < seg_end[-1]).astype(jnp.int32)

    xs = _dispatch(x, slot1, slot2, jnp.zeros((n_tiles * MOE_TM, D_MODEL), F32))
    ys = _experts(xs, tile_expert.astype(jnp.int32), tile_valid, wg, wu, wd)
    return _combine(x, ys, slot1, slot2, wts, g, b)


def _pad_lanes(v):
    return jnp.pad(v, ((0, 0), (0, LANES - v.shape[-1])))


def _mixer(x, xb, bsz, seqlen, w_in_all, layer, conv_dw_w, conv_dw_b, conv_ln_g, conv_ln_b, conv_w_out,
           ssm_conv_w, ssm_conv_b, ssm_dt_bias, ssm_a_log, ssm_d, ssm_norm_w, ssm_w_out, w_out, ln_g, ln_b):
    proj = _matmul(xb, w_in_all, layer=layer, n_blocks=MAIN_W // 1024, tn=1024, out_dtype=BF16)
    dt_raw = _matmul(xb, _pad_lanes(w_in_all[layer, :, COL_DT:COL_GATES]), n_blocks=1, tn=LANES, out_dtype=F32)
    gates = _matmul(xb, w_in_all[layer, :, COL_GATES:], n_blocks=2, tn=1024, out_dtype=BF16)
    y_n = _ssd_branch(proj, dt_raw, ssm_conv_w, ssm_conv_b[None], _pad_lanes(ssm_dt_bias[None]),
                      _pad_lanes(ssm_a_log[None]), jnp.repeat(ssm_d, HEADDIM)[None], ssm_norm_w[None],
                      bsz, seqlen)
    return _mix(x, proj, y_n, gates, conv_dw_w, conv_dw_b[None], conv_ln_g[None], conv_ln_b[None],
                conv_w_out.astype(BF16), ssm_w_out.astype(BF16), w_out.astype(BF16), ln_g[None], ln_b[None],
                bsz, seqlen)


def kernel(x, mix_w_in, conv_dw_w, conv_dw_b, conv_ln_g, conv_ln_b, conv_w_out, ssm_conv_w, ssm_conv_b,
           ssm_dt_bias, ssm_a_log, ssm_d, ssm_norm_w, ssm_w_out, mix_w_out, ln_mix_g, ln_mix_b,
           ffn_w_gate, ffn_w_up, ffn_w_down, moe_router, moe_w_gate, moe_w_up, moe_w_down,
           ln_ffn_g, ln_ffn_b):
    bsz, seqlen, d = x.shape
    xf = x.reshape(bsz * seqlen, d)
    xb = xf
    for i in range(DEPTH):
        xf, xb = _mixer(xf, xb, bsz, seqlen, mix_w_in, i, conv_dw_w[i], conv_dw_b[i], conv_ln_g[i],
                        conv_ln_b[i], conv_w_out[i], ssm_conv_w[i], ssm_conv_b[i], ssm_dt_bias[i],
                        ssm_a_log[i], ssm_d[i], ssm_norm_w[i], ssm_w_out[i], mix_w_out[i],
                        ln_mix_g[i], ln_mix_b[i])
        j = i // 2
        if i % 2 == 0:
            xf, xb = _ffn(xf, xb, ffn_w_gate[j].astype(BF16), ffn_w_up[j].astype(BF16),
                          ffn_w_down[j].astype(BF16), ln_ffn_g[i][None], ln_ffn_b[i][None])
        else:
            xf = _moe(xf, _pad_lanes(moe_router[j]), moe_w_gate[j].astype(BF16), moe_w_up[j].astype(BF16),
                      moe_w_down[j].astype(BF16), ln_ffn_g[i][None], ln_ffn_b[i][None])
            xb = xf.astype(BF16)
    return xf.reshape(bsz, seqlen, d)
```

```python
import functools
import math

import jax
import jax.numpy as jnp
from jax import lax
from jax.experimental import pallas as pl
from jax.experimental.pallas import tpu as pltpu

F32 = jnp.float32
BF16 = jnp.bfloat16

D_MODEL = 1024
DEPTH = 2
CONV_K = 31
D_INNER = 2048
HEADDIM = 64
HEADS = D_INNER // HEADDIM
GROUPS = 4
HEADS_PER_GROUP = HEADS // GROUPS
STATE = 128
SSM_K = 4
CHUNK = 128
GROUP_W = D_INNER // GROUPS
BC_W = 2 * GROUPS * STATE
N_EXPERTS = 8
LN_EPS = 1e-5
RMS_EPS = 1e-5
ALPHA = (2 * DEPTH) ** 0.25

COL_Z = 2 * D_MODEL
COL_XS = COL_Z + D_INNER
COL_BC = COL_XS + D_INNER
COL_DT = COL_BC + BC_W
COL_GATES = COL_DT + HEADS
MAIN_W = COL_DT

LANES = 128
SUBLANES = 8
VMEM_LIMIT = 56 * 1024 * 1024


def _params(sem):
    return pltpu.CompilerParams(dimension_semantics=sem, vmem_limit_bytes=VMEM_LIMIT)


def _layer_norm(v, g, b):
    mu = jnp.mean(v, axis=-1, keepdims=True)
    d = v - mu
    var = jnp.mean(d * d, axis=-1, keepdims=True)
    return d * lax.rsqrt(var + LN_EPS) * g + b


def _sigmoid(v):
    return 0.5 + 0.5 * jnp.tanh(0.5 * v)


def _silu(v):
    half = 0.5 * v
    return half + half * jnp.tanh(half)


def _mm_body(x_ref, w_ref, o_ref, wb):
    @pl.when(pl.program_id(1) == 0)
    def _():
        wb[...] = w_ref[...].astype(BF16)

    o_ref[...] = jnp.dot(x_ref[...].astype(BF16), wb[...], preferred_element_type=F32).astype(o_ref.dtype)


def _matmul(x, w, *, n_blocks, tn, out_dtype, layer=None, tm=1024):
    t, k = x.shape
    tm = min(tm, t)
    if layer is None:
        w_spec = pl.BlockSpec((k, tn), lambda n, m: (0, n))
    else:
        w = w.reshape(w.shape[0] * k, w.shape[2])
        w_spec = pl.BlockSpec((k, tn), lambda n, m: (layer, n))
    return pl.pallas_call(
        _mm_body,
        grid=(n_blocks, t // tm),
        in_specs=[
            pl.BlockSpec((tm, k), lambda n, m: (m, 0)),
            w_spec,
        ],
        out_specs=pl.BlockSpec((tm, tn), lambda n, m: (m, n)),
        out_shape=jax.ShapeDtypeStruct((t, n_blocks * tn), out_dtype),
        scratch_shapes=[pltpu.VMEM((k, tn), BF16)],
        compiler_params=_params(("arbitrary", "arbitrary")),
        name="proj_matmul",
    )(x, w)


SSD_LANE_BLK = 512
SSD_HALO = 16


def _conv4_silu(cur_ref, ext, w_ref, b_ref, c, store):
    width = cur_ref.shape[1]

    @pl.when(c == 0)
    def _():
        ext[0:SSD_HALO, :] = jnp.zeros((SSD_HALO, width), BF16)

    @pl.when(c > 0)
    def _():
        ext[0:SSD_HALO, :] = ext[CHUNK:CHUNK + SSD_HALO, :]

    ext[SSD_HALO:SSD_HALO + CHUNK, :] = cur_ref[...]
    n_shift = SSM_K - 1
    r = lax.broadcasted_iota(jnp.int32, (n_shift * CHUNK, SSD_HALO + CHUNK), 0)
    col = lax.broadcasted_iota(jnp.int32, (n_shift * CHUNK, SSD_HALO + CHUNK), 1)
    sel = jnp.zeros(r.shape, F32)
    for k in range(n_shift):
        in_rows = jnp.logical_and(r >= k * CHUNK, r < (k + 1) * CHUNK)
        hit = jnp.logical_and(in_rows, col == r - k * CHUNK + SSD_HALO - n_shift + k)
        sel = jnp.where(hit, 1.0, sel)
    sel = sel.astype(BF16)
    for i in range(width // SSD_LANE_BLK):
        ls = slice(i * SSD_LANE_BLK, (i + 1) * SSD_LANE_BLK)
        shifted = jnp.dot(sel, ext[:, ls], preferred_element_type=F32)
        out = b_ref[:, ls] + w_ref[n_shift:n_shift + 1, ls] * ext[SSD_HALO:SSD_HALO + CHUNK, ls].astype(F32)
        for k in range(n_shift):
            out = out + w_ref[k:k + 1, ls] * shifted[k * CHUNK:(k + 1) * CHUNK, :]
        store(ls, _silu(out))


def _ssd_body(xs_ref, bc_ref, z_ref, dt_ref, cwx_ref, cbx_ref, cwbc_ref, cbbc_ref, dtb_ref, alog_ref,
              dvec_ref, nw_ref, o_ref, ext_x, ext_bc, xs_f, xs_b, bc_f, state, ybuf):
    c = pl.program_id(1)

    @pl.when(c == 0)
    def _():
        state[...] = jnp.zeros_like(state)

    def store_xs(ls, v):
        xs_f[:, ls] = v
        xs_b[:, ls] = v.astype(BF16)

    def store_bc(ls, v):
        bc_f[:, ls] = v

    _conv4_silu(xs_ref, ext_x, cwx_ref, cbx_ref, c, store_xs)
    _conv4_silu(bc_ref, ext_bc, cwbc_ref, cbbc_ref, c, store_bc)

    dt_raw = dt_ref[...] + dtb_ref[...]
    dt = jnp.maximum(dt_raw, 0.0) + jnp.log1p(jnp.exp(-jnp.abs(dt_raw)))
    da = dt * (-jnp.exp(alog_ref[...]))
    row = lax.broadcasted_iota(jnp.int32, (CHUNK, CHUNK), 0)
    lane = lax.broadcasted_iota(jnp.int32, (CHUNK, CHUNK), 1)
    causal = row >= lane
    first_head = lane < HEADDIM
    acum = jnp.dot(causal.astype(F32), da, preferred_element_type=F32,
                   precision=lax.Precision.HIGHEST)
    eacum = jnp.exp(acum)
    acum_t = acum.T
    dt_t = dt.T
    src_t = acum_t - jnp.log(dt_t)
    last = acum_t[:, CHUNK - 1:CHUNK]
    wt_t = jnp.exp(last - acum_t) * dt_t
    chunk_decay = eacum[CHUNK - 1:CHUNK, :]

    def block_diag(v):
        zero = jnp.zeros_like(v)
        return jnp.concatenate([jnp.where(first_head, v, zero), jnp.where(first_head, zero, v)], axis=0)

    for g in range(GROUPS):
        bg = bc_f[:, g * STATE:(g + 1) * STATE]
        cg = bc_f[:, GROUPS * STATE + g * STATE:GROUPS * STATE + (g + 1) * STATE]
        cb = lax.dot_general(cg.astype(BF16), bg.astype(BF16), (((1,), (1,)), ((), ())),
                             preferred_element_type=F32)
        bg_t = bg.T
        for q in range(HEADS_PER_GROUP // 2):
            pair = (g * HEADS_PER_GROUP) // 2 + q
            ps = slice(pair * 2 * HEADDIM, (pair + 1) * 2 * HEADDIM)
            x_bd = block_diag(xs_b[:, ps])
            prev = state[pair]
            s_bd = block_diag(prev.astype(BF16))
            m_parts, ce_parts, bw_parts = [], [], []
            for h in (2 * pair, 2 * pair + 1):
                colb = jnp.broadcast_to(acum[:, h:h + 1], (CHUNK, CHUNK))
                ecolb = jnp.broadcast_to(eacum[:, h:h + 1], (CHUNK, CHUNK))
                rowb = jnp.broadcast_to(src_t[h:h + 1, :], (CHUNK, CHUNK))
                m_parts.append((cb * jnp.exp(jnp.where(causal, colb - rowb, -jnp.inf))).astype(BF16))
                ce_parts.append((cg * ecolb).astype(BF16))
                bw_parts.append((bg_t * wt_t[h:h + 1, :]).astype(BF16))
            y2 = jnp.dot(jnp.concatenate(m_parts + ce_parts, axis=1), jnp.concatenate([x_bd, s_bd], axis=0),
                         preferred_element_type=F32)
            s2 = jnp.dot(jnp.concatenate(bw_parts, axis=1), x_bd, preferred_element_type=F32)
            h0 = 2 * pair
            cd2 = jnp.where(first_head[0:1, :], chunk_decay[:, h0:h0 + 1], chunk_decay[:, h0 + 1:h0 + 2])
            state[pair] = prev * cd2 + s2
            ybuf[:, ps] = y2

    for g in range(GROUPS):
        gs = slice(g * GROUP_W, (g + 1) * GROUP_W)
        y = ybuf[:, gs] + dvec_ref[:, gs] * xs_f[:, gs]
        hg = y * _silu(z_ref[:, gs].astype(F32))
        ms = jnp.mean(hg * hg, axis=-1, keepdims=True)
        o_ref[:, gs] = (hg * lax.rsqrt(ms + RMS_EPS) * nw_ref[:, gs]).astype(o_ref.dtype)


def _ssd_branch(proj, dt_raw, cw, cb, dtb, alog, dvec, nw, bsz, seqlen):
    nc = seqlen // CHUNK
    row = lambda b, c: b * nc + c
    const = lambda b, c: (0, 0)
    xs_blk = COL_XS // D_INNER
    bc_blk = COL_BC // BC_W
    z_blk = COL_Z // D_INNER
    return pl.pallas_call(
        _ssd_body,
        grid=(bsz, nc),
        in_specs=[
            pl.BlockSpec((CHUNK, D_INNER), lambda b, c: (row(b, c), xs_blk)),
            pl.BlockSpec((CHUNK, BC_W), lambda b, c: (row(b, c), bc_blk)),
            pl.BlockSpec((CHUNK, D_INNER), lambda b, c: (row(b, c), z_blk)),
            pl.BlockSpec((CHUNK, LANES), lambda b, c: (row(b, c), 0)),
            pl.BlockSpec((SSM_K, D_INNER), lambda b, c: (0, 0)),
            pl.BlockSpec((1, D_INNER), lambda b, c: (0, 0)),
            pl.BlockSpec((SSM_K, BC_W), lambda b, c: (0, D_INNER // BC_W)),
            pl.BlockSpec((1, BC_W), lambda b, c: (0, D_INNER // BC_W)),
            pl.BlockSpec((1, LANES), const),
            pl.BlockSpec((1, LANES), const),
            pl.BlockSpec((1, D_INNER), const),
            pl.BlockSpec((1, D_INNER), const),
        ],
        out_specs=pl.BlockSpec((CHUNK, D_INNER), lambda b, c: (row(b, c), 0)),
        out_shape=jax.ShapeDtypeStruct((bsz * seqlen, D_INNER), BF16),
        scratch_shapes=[
            pltpu.VMEM((SSD_HALO + CHUNK, D_INNER), BF16),
            pltpu.VMEM((SSD_HALO + CHUNK, BC_W), BF16),
            pltpu.VMEM((CHUNK, D_INNER), F32),
            pltpu.VMEM((CHUNK, D_INNER), BF16),
            pltpu.VMEM((CHUNK, BC_W), F32),
            pltpu.VMEM((HEADS // 2, STATE, 2 * HEADDIM), F32),
            pltpu.VMEM((CHUNK, D_INNER), F32),
        ],
        compiler_params=_params(("arbitrary", "arbitrary")),
        name="ssd_branch",
    )(proj, proj, proj, dt_raw, cw, cb, cw, cb, dtb, alog, dvec, nw)


CONV_TL = 256
CONV_HALO = 32
CONV_RB = 64


def _mix_body(x_ref, cv_ref, cg_ref, yn_ref, gt_ref, dww_ref, dwb_ref, lng_ref, lnb_ref, wconv_ref,
              wssm_ref, wout_ref, g_ref, b_ref, o_ref, ob_ref, ubuf, cbuf):
    l = pl.program_id(1)

    @pl.when(l == 0)
    def _():
        ubuf[0, 0:CONV_HALO, :] = jnp.zeros((CONV_HALO, D_MODEL), F32)

    @pl.when(l > 0)
    def _():
        ubuf[0, 0:CONV_HALO, :] = ubuf[0, CONV_TL:CONV_TL + CONV_HALO, :]

    y_ssm = jnp.dot(yn_ref[...], wssm_ref[...], preferred_element_type=F32)

    cv = cv_ref[...].astype(F32)
    cg = cg_ref[...].astype(F32)
    ubuf[0, CONV_HALO:CONV_HALO + CONV_TL, :] = cv * _sigmoid(cg)

    nshift = CONV_HALO + CONV_TL - SUBLANES
    for c in range(D_MODEL // LANES):
        cs = slice(c * LANES, (c + 1) * LANES)
        for j in range(1, SUBLANES):
            ubuf[j, 0:nshift, cs] = ubuf[0, j:j + nshift, cs]

    base = CONV_HALO - (CONV_K - 1)
    for c in range(D_MODEL // LANES):
        cs = slice(c * LANES, (c + 1) * LANES)
        wc = dww_ref[:, cs]
        bc = dwb_ref[:, cs]
        for r in range(CONV_TL // CONV_RB):
            acc = jnp.broadcast_to(bc, (CONV_RB, LANES))
            for k in range(CONV_K):
                j = (base + k) % SUBLANES
                r0 = base + k - j + r * CONV_RB
                acc = acc + wc[k:k + 1, :] * ubuf[j, r0:r0 + CONV_RB, cs]
            cbuf[r * CONV_RB:(r + 1) * CONV_RB, cs] = acc

    v = _layer_norm(cbuf[...], lng_ref[...], lnb_ref[...])
    y_conv = jnp.dot(_silu(v).astype(BF16), wconv_ref[...], preferred_element_type=F32)

    gates = gt_ref[...].astype(F32)
    hmix = _sigmoid(gates[:, :D_MODEL]) * y_conv + _sigmoid(gates[:, D_MODEL:]) * y_ssm
    mix = jnp.dot(hmix.astype(BF16), wout_ref[...], preferred_element_type=F32)
    out = _layer_norm(ALPHA * x_ref[...] + mix, g_ref[...], b_ref[...])
    o_ref[...] = out
    ob_ref[...] = out.astype(BF16)


def _mix(x, proj, y_n, gates, dww, dwb, lng, lnb, wconv, wssm, wout, g, b, bsz, seqlen):
    nl = seqlen // CONV_TL
    t = bsz * seqlen
    rowb = lambda w, col=0: pl.BlockSpec((CONV_TL, w), lambda bb, l: (bb * nl + l, col))
    full = lambda a: pl.BlockSpec(a.shape, lambda bb, l: (0, 0), pipeline_mode=pl.Buffered(1))
    return pl.pallas_call(
        _mix_body,
        grid=(bsz, nl),
        in_specs=[rowb(D_MODEL), rowb(D_MODEL, 0), rowb(D_MODEL, 1), rowb(D_INNER), rowb(2 * D_MODEL),
                  full(dww), full(dwb), full(lng), full(lnb), full(wconv), full(wssm), full(wout),
                  full(g), full(b)],
        out_specs=[rowb(D_MODEL), rowb(D_MODEL)],
        out_shape=[jax.ShapeDtypeStruct((t, D_MODEL), F32), jax.ShapeDtypeStruct((t, D_MODEL), BF16)],
        scratch_shapes=[
            pltpu.VMEM((SUBLANES, CONV_HALO + CONV_TL, D_MODEL), F32),
            pltpu.VMEM((CONV_TL, D_MODEL), F32),
        ],
        compiler_params=_params(("arbitrary", "arbitrary")),
        name="conv_merge_ln",
    )(x, proj, proj, y_n, gates, dww, dwb, lng, lnb, wconv, wssm, wout, g, b)


def _ffn_body(x_ref, xb_ref, wg_ref, wu_ref, wd_ref, g_ref, b_ref, o_ref, ob_ref):
    xb = xb_ref[...]
    hg = jnp.dot(xb, wg_ref[...], preferred_element_type=F32)
    hu = jnp.dot(xb, wu_ref[...], preferred_element_type=F32)
    f = jnp.dot((_silu(hg) * hu).astype(BF16), wd_ref[...], preferred_element_type=F32)
    out = _layer_norm(ALPHA * x_ref[...] + f, g_ref[...], b_ref[...])
    o_ref[...] = out
    ob_ref[...] = out.astype(BF16)


def _ffn(x, xb, wg, wu, wd, g, b, tm=512):
    t = x.shape[0]
    tm = min(tm, t)
    rowb = lambda w: pl.BlockSpec((tm, w), lambda m: (m, 0))
    full = lambda a: pl.BlockSpec(a.shape, lambda m: (0, 0), pipeline_mode=pl.Buffered(1))
    return pl.pallas_call(
        _ffn_body,
        grid=(t // tm,),
        in_specs=[rowb(D_MODEL), rowb(D_MODEL), full(wg), full(wu), full(wd), full(g), full(b)],
        out_specs=[rowb(D_MODEL), rowb(D_MODEL)],
        out_shape=[jax.ShapeDtypeStruct((t, D_MODEL), F32), jax.ShapeDtypeStruct((t, D_MODEL), BF16)],
        compiler_params=_params(("arbitrary",)),
        name="ffn_ln",
    )(x, xb, wg, wu, wd, g, b)


def _router_body(x_ref, wr_ref, wts_ref, meta_ref, cnt_ref, carry):
    @pl.when(pl.program_id(0) == 0)
    def _():
        carry[...] = jnp.zeros_like(carry)

    logits = jnp.dot(x_ref[...], wr_ref[...], preferred_element_type=F32, precision=lax.Precision.HIGHEST)
    tm = logits.shape[0]
    lane = lax.broadcasted_iota(jnp.int32, logits.shape, 1)
    logits = jnp.where(lane < N_EXPERTS, logits, -jnp.inf)
    m1 = jnp.max(logits, axis=-1, keepdims=True)
    i1 = jnp.min(jnp.where(logits == m1, lane, LANES), axis=-1, keepdims=True)
    rest = jnp.where(lane == i1, -jnp.inf, logits)
    m2 = jnp.max(rest, axis=-1, keepdims=True)
    i2 = jnp.min(jnp.where(rest == m2, lane, LANES), axis=-1, keepdims=True)
    e2 = jnp.exp(m2 - m1)
    denom = 1.0 + e2
    wts_ref[...] = jnp.where(lane == 0, 1.0 / denom, jnp.where(lane == 1, e2 / denom, 0.0))

    chosen = jnp.logical_or(lane == i1, lane == i2)
    chosen_f = jnp.where(chosen, 1.0, 0.0)
    r = lax.broadcasted_iota(jnp.int32, (tm, tm), 0)
    c = lax.broadcasted_iota(jnp.int32, (tm, tm), 1)
    earlier = jnp.where(c < r, 1.0, 0.0).astype(BF16)
    prefix = jnp.dot(earlier, chosen_f.astype(BF16), preferred_element_type=F32) + carry[...]
    rank1 = jnp.sum(jnp.where(lane == i1, prefix, 0.0), axis=-1, keepdims=True).astype(jnp.int32)
    rank2 = jnp.sum(jnp.where(lane == i2, prefix, 0.0), axis=-1, keepdims=True).astype(jnp.int32)
    meta = jnp.where(lane == 0, i1, jnp.where(lane == 1, i2,
                     jnp.where(lane == 2, rank1, jnp.where(lane == 3, rank2, 0))))
    meta_ref[...] = meta.T[0:SUBLANES, :]
    total = carry[...] + jnp.sum(chosen_f, axis=0, keepdims=True)
    carry[...] = total
    cnt_ref[...] = jnp.broadcast_to(total, cnt_ref.shape).astype(jnp.int32)


def _router(x, wr_pad, tm=1024):
    t = x.shape[0]
    tm = min(tm, t)
    return pl.pallas_call(
        _router_body,
        grid=(t // tm,),
        in_specs=[pl.BlockSpec((tm, D_MODEL), lambda m: (m, 0)),
                  pl.BlockSpec((D_MODEL, LANES), lambda m: (0, 0))],
        out_specs=[pl.BlockSpec((tm, LANES), lambda m: (m, 0)),
                   pl.BlockSpec((SUBLANES, tm), lambda m: (0, m)),
                   pl.BlockSpec((SUBLANES, LANES), lambda m: (0, 0))],
        out_shape=[jax.ShapeDtypeStruct((t, LANES), F32),
                   jax.ShapeDtypeStruct((SUBLANES, t), jnp.int32),
                   jax.ShapeDtypeStruct((SUBLANES, LANES), jnp.int32)],
        scratch_shapes=[pltpu.VMEM((1, LANES), F32)],
        compiler_params=_params(("arbitrary",)),
        name="router",
    )(x, wr_pad)


MOE_FF_SPLIT = 2
MOE_TM = 512
MOE_ROWS = 512
DMA_UNROLL = 8


def _row_copy(src, dst, src_row, dst_row, sem):
    return pltpu.make_async_copy(src.at[pl.ds(src_row, 1)], dst.at[pl.ds(dst_row, 1)], sem)


def _dispatch_body(slot1_ref, slot2_ref, x_ref, xs_in_hbm, xs_hbm, sem):
    del xs_in_hbm
    n = slot1_ref.shape[0]

    def issue(t, carry):
        _row_copy(x_ref, xs_hbm, t, slot1_ref[t], sem).start()
        _row_copy(x_ref, xs_hbm, t, slot2_ref[t], sem).start()
        return carry

    lax.fori_loop(0, n, issue, 0, unroll=DMA_UNROLL)

    def drain(t, carry):
        _row_copy(x_ref, xs_hbm, 0, 0, sem).wait()
        _row_copy(x_ref, xs_hbm, 0, 0, sem).wait()
        return carry

    lax.fori_loop(0, n, drain, 0, unroll=DMA_UNROLL)


def _dispatch(x, slot1, slot2, xs_zero):
    t = x.shape[0]
    n = min(MOE_ROWS, t)
    smem = lambda: pl.BlockSpec((n,), lambda i: (i,), memory_space=pltpu.SMEM)
    return pl.pallas_call(
        _dispatch_body,
        grid=(t // n,),
        in_specs=[smem(), smem(), pl.BlockSpec((n, D_MODEL), lambda i: (i, 0)),
                  pl.BlockSpec(memory_space=pl.ANY)],
        out_specs=pl.BlockSpec(memory_space=pl.ANY),
        out_shape=jax.ShapeDtypeStruct(xs_zero.shape, xs_zero.dtype),
        scratch_shapes=[pltpu.SemaphoreType.DMA(())],
        input_output_aliases={3: 0},
        compiler_params=_params(("arbitrary",)),
        name="moe_dispatch",
    )(slot1, slot2, x, xs_zero)


def _experts_body(te_ref, valid_ref, xs_ref, wg_ref, wu_ref, wd_ref, o_ref, xb):
    j = pl.program_id(0)
    f = pl.program_id(1)

    @pl.when(f == 0)
    def _():
        xb[...] = xs_ref[...].astype(BF16)
        o_ref[...] = jnp.zeros_like(o_ref)

    @pl.when(valid_ref[j] != 0)
    def _():
        hg = jnp.dot(xb[...], wg_ref[0], preferred_element_type=F32)
        hu = jnp.dot(xb[...], wu_ref[0], preferred_element_type=F32)
        o_ref[...] += jnp.dot((_silu(hg) * hu).astype(BF16), wd_ref[0], preferred_element_type=F32)


def _experts(xs, tile_expert, tile_valid, wg, wu, wd):
    rows = xs.shape[0]
    tm = MOE_TM
    ffs = wg.shape[-1] // MOE_FF_SPLIT
    grid_spec = pltpu.PrefetchScalarGridSpec(
        num_scalar_prefetch=2,
        grid=(rows // tm, MOE_FF_SPLIT),
        in_specs=[
            pl.BlockSpec((tm, D_MODEL), lambda j, f, te, va: (j, 0)),
            pl.BlockSpec((1, D_MODEL, ffs), lambda j, f, te, va: (te[j], 0, f)),
            pl.BlockSpec((1, D_MODEL, ffs), lambda j, f, te, va: (te[j], 0, f)),
            pl.BlockSpec((1, ffs, D_MODEL), lambda j, f, te, va: (te[j], f, 0)),
        ],
        out_specs=pl.BlockSpec((tm, D_MODEL), lambda j, f, te, va: (j, 0)),
        scratch_shapes=[pltpu.VMEM((tm, D_MODEL), BF16)],
    )
    return pl.pallas_call(
        _experts_body,
        grid_spec=grid_spec,
        out_shape=jax.ShapeDtypeStruct((rows, D_MODEL), F32),
        compiler_params=_params(("arbitrary", "arbitrary")),
        name="moe_experts",
    )(tile_expert, tile_valid, xs, wg, wu, wd)


def _combine_body(slot1_ref, slot2_ref, x_ref, wts_ref, g_ref, b_ref, y_hbm, o_ref, y1, y2, sem):
    n = slot1_ref.shape[0]

    def issue(t, carry):
        _row_copy(y_hbm, y1, slot1_ref[t], t, sem).start()
        _row_copy(y_hbm, y2, slot2_ref[t], t, sem).start()
        return carry

    lax.fori_loop(0, n, issue, 0, unroll=DMA_UNROLL)

    def drain(t, carry):
        _row_copy(y_hbm, y1, 0, 0, sem).wait()
        _row_copy(y_hbm, y2, 0, 0, sem).wait()
        return carry

    lax.fori_loop(0, n, drain, 0, unroll=DMA_UNROLL)

    w = wts_ref[...]
    f = w[:, 0:1] * y1[...] + w[:, 1:2] * y2[...]
    o_ref[...] = _layer_norm(ALPHA * x_ref[...] + f, g_ref[...], b_ref[...])


def _combine(x, ys, slot1, slot2, wts, g, b):
    t = x.shape[0]
    n = min(MOE_ROWS, t)
    smem = lambda: pl.BlockSpec((n,), lambda i: (i,), memory_space=pltpu.SMEM)
    rowb = lambda w: pl.BlockSpec((n, w), lambda i: (i, 0))
    full = lambda a: pl.BlockSpec(a.shape, lambda i: (0, 0))
    return pl.pallas_call(
        _combine_body,
        grid=(t // n,),
        in_specs=[smem(), smem(), rowb(D_MODEL), rowb(LANES), full(g), full(b),
                  pl.BlockSpec(memory_space=pl.ANY)],
        out_specs=rowb(D_MODEL),
        out_shape=jax.ShapeDtypeStruct((t, D_MODEL), F32),
        scratch_shapes=[pltpu.VMEM((n, D_MODEL), F32), pltpu.VMEM((n, D_MODEL), F32),
                        pltpu.SemaphoreType.DMA(())],
        compiler_params=_params(("arbitrary",)),
        name="moe_combine_ln",
    )(slot1, slot2, x, wts, g, b, ys)


def _moe(x, wr_pad, wg, wu, wd, g, b):
    t = x.shape[0]
    wts, meta, cnt = _router(x, wr_pad)
    counts = cnt[0, :N_EXPERTS]
    padded = ((counts + MOE_TM - 1) // MOE_TM) * MOE_TM
    seg_end = jnp.cumsum(padded)
    seg_off = seg_end - padded
    def slot(idx, rank):
        for e in range(N_EXPERTS):
            rank = rank + jnp.where(idx == e, seg_off[e], 0)
        return rank.astype(jnp.int32)

    slot1 = slot(meta[0], meta[2])
    slot2 = slot(meta[1], meta[3])
    n_tiles = (2 * t) // MOE_TM + N_EXPERTS
    tile_start = jnp.arange(n_tiles, dtype=jnp.int32) * MOE_TM
    tile_expert = jnp.minimum(jnp.sum(tile_start[:, None] >= seg_end[None, :], axis=1), N_EXPERTS - 1)
    tile_valid = (tile_start < seg_end[-1]).astype(jnp.int32)

    xs = _dispatch(x, slot1, slot2, jnp.zeros((n_tiles * MOE_TM, D_MODEL), F32))
    ys = _experts(xs, tile_expert.astype(jnp.int32), tile_valid, wg, wu, wd)
    return _combine(x, ys, slot1, slot2, wts, g, b)


def _pad_lanes(v):
    return jnp.pad(v, ((0, 0), (0, LANES - v.shape[-1])))


def _mixer(x, xb, bsz, seqlen, w_in_all, layer, conv_dw_w, conv_dw_b, conv_ln_g, conv_ln_b, conv_w_out,
           ssm_conv_w, ssm_conv_b, ssm_dt_bias, ssm_a_log, ssm_d, ssm_norm_w, ssm_w_out, w_out, ln_g, ln_b):
    proj = _matmul(xb, w_in_all, layer=layer, n_blocks=MAIN_W // 1024, tn=1024, out_dtype=BF16)
    dt_raw = _matmul(xb, _pad_lanes(w_in_all[layer, :, COL_DT:COL_GATES]), n_blocks=1, tn=LANES, out_dtype=F32)
    gates = _matmul(xb, w_in_all[layer, :, COL_GATES:], n_blocks=2, tn=1024, out_dtype=BF16)
    y_n = _ssd_branch(proj, dt_raw, ssm_conv_w, ssm_conv_b[None], _pad_lanes(ssm_dt_bias[None]),
                      _pad_lanes(ssm_a_log[None]), jnp.repeat(ssm_d, HEADDIM)[None], ssm_norm_w[None],
                      bsz, seqlen)
    return _mix(x, proj, y_n, gates, conv_dw_w, conv_dw_b[None], conv_ln_g[None], conv_ln_b[None],
                conv_w_out.astype(BF16), ssm_w_out.astype(BF16), w_out.astype(BF16), ln_g[None], ln_b[None],
                bsz, seqlen)


def kernel(x, mix_w_in, conv_dw_w, conv_dw_b, conv_ln_g, conv_ln_b, conv_w_out, ssm_conv_w, ssm_conv_b,
           ssm_dt_bias, ssm_a_log, ssm_d, ssm_norm_w, ssm_w_out, mix_w_out, ln_mix_g, ln_mix_b,
           ffn_w_gate, ffn_w_up, ffn_w_down, moe_router, moe_w_gate, moe_w_up, moe_w_down,
           ln_ffn_g, ln_ffn_b):
    bsz, seqlen, d = x.shape
    xf = x.reshape(bsz * seqlen, d)
    xb = xf
    for i in range(DEPTH):
        xf, xb = _mixer(xf, xb, bsz, seqlen, mix_w_in, i, conv_dw_w[i], conv_dw_b[i], conv_ln_g[i],
                        conv_ln_b[i], conv_w_out[i], ssm_conv_w[i], ssm_conv_b[i], ssm_dt_bias[i],
                        ssm_a_log[i], ssm_d[i], ssm_norm_w[i], ssm_w_out[i], mix_w_out[i],
                        ln_mix_g[i], ln_mix_b[i])
        j = i // 2
        if i % 2 == 0:
            xf, xb = _ffn(xf, xb, ffn_w_gate[j].astype(BF16), ffn_w_up[j].astype(BF16),
                          ffn_w_down[j].astype(BF16), ln_ffn_g[i][None], ln_ffn_b[i][None])
        else:
            xf = _moe(xf, _pad_lanes(moe_router[j]), moe_w_gate[j].astype(BF16), moe_w_up[j].astype(BF16),
                      moe_w_down[j].astype(BF16), ln_ffn_g[i][None], ln_ffn_b[i][None])
            xb = xf.astype(BF16)
    return xf.reshape(bsz, seqlen, d)
```
